```python
import jax, jax.numpy as jnp
from jax import lax
import numpy as np

D_MODEL = 1024
BATCH = 16
SEQ = 2048
DEPTH = 4

N_META = 16
GLA_HEADS = 4
GLA_K = D_MODEL // 2
GLA_V = D_MODEL
GLA_DK = GLA_K // GLA_HEADS
GLA_DV = GLA_V // GLA_HEADS
GATE_RANK = 16
GATE_TAU = 16.0
CHUNK = 64
CONV_DIM = D_MODEL
CONV_GROUPS = 8
CONV_K = 3
EPS = 1e-6
IN_SIZES = (GLA_K, GLA_K, GLA_V, GLA_V, GATE_RANK,
            CONV_DIM, CONV_DIM, CONV_DIM, CONV_DIM,
            D_MODEL, D_MODEL)
N_IN = GLA_K * 2 + GLA_V * 2 + GATE_RANK + CONV_DIM * 4 + D_MODEL * 2

kernel_name = "hybrid_gla_shortconv_gated_merge"


def rmsnorm(x, g):
    x32 = x.astype(jnp.float32)
    y = x32 * lax.rsqrt(jnp.mean(x32 * x32, axis=-1, keepdims=True) + EPS) * g.astype(jnp.float32)
    return y.astype(x.dtype)


def gla_chunked(q, k, v, g_log):
    bsz, L, H, DK = q.shape
    DV = v.shape[-1]
    front = (-N_META) % CHUNK
    back = (-(L + front)) % CHUNK
    Lp = L + front + back
    n_chunks = Lp // CHUNK

    def to_chunks(t):
        t = jnp.pad(t.astype(jnp.float32), ((0, 0), (front, back), (0, 0), (0, 0)))
        return t.reshape(bsz, n_chunks, CHUNK, H, t.shape[-1]).transpose(0, 3, 1, 2, 4)

    q, k, v, g_log = to_chunks(q), to_chunks(k), to_chunks(v), to_chunks(g_log)
    b = jnp.cumsum(g_log, axis=3)
    b_last = b[:, :, :, -1:, :]
    q_in = q * jnp.exp(b)
    k_in = k * jnp.exp(-b)
    k_st = k * jnp.exp(b_last - b)

    causal = jnp.tril(jnp.ones((CHUNK, CHUNK), dtype=bool))
    att = jnp.einsum('bhncd,bhnsd->bhncs', q_in, k_in)
    att = jnp.where(causal, att, 0.0)
    o_intra = jnp.einsum('bhncs,bhnse->bhnce', att, v)

    decay = jnp.exp(b_last[:, :, :, 0, :])

    def step(state, xs):
        q_n, k_n, v_n, d_n = xs
        o_n = jnp.einsum('bhcd,bhde->bhce', q_n, state)
        state = state * d_n[..., None] + jnp.einsum('bhcd,bhce->bhde', k_n, v_n)
        return state, o_n

    s0 = jnp.zeros((bsz, H, DK, DV), jnp.float32)
    xs = (jnp.moveaxis(q_in, 2, 0), jnp.moveaxis(k_st, 2, 0), jnp.moveaxis(v, 2, 0), jnp.moveaxis(decay, 2, 0))
    _, o_inter = lax.scan(step, s0, xs)
    o = o_intra + jnp.moveaxis(o_inter, 0, 2)
    o = o.transpose(0, 2, 3, 1, 4).reshape(bsz, Lp, H, DV)
    return o[:, front:front + L]


def causal_dwconv(u, w):
    return lax.conv_general_dilated(
        u, w[:, None, :].astype(u.dtype), window_strides=(1,), padding=[(CONV_K - 1, 0)],
        dimension_numbers=('NWC', 'WIO', 'NWC'), feature_group_count=u.shape[-1])


def hybrid_layer(x, norm_g, w_in, w_gate_up, b_gate, gla_norm_g, w_o_gla, conv_w, w_o_conv, w_out):
    bsz, L, _ = x.shape
    h = rmsnorm(x, norm_g)
    p = h @ w_in
    offsets = [int(o) for o in np.cumsum(IN_SIZES)[:-1]]
    q, k, v, r, glr, ch, cb, cc, cz, ga, gb = jnp.split(p, offsets, axis=-1)

    g_log = jax.nn.log_sigmoid((glr @ w_gate_up + b_gate).astype(jnp.float32)) / GATE_TAU
    q = q.reshape(bsz, L, GLA_HEADS, GLA_DK) * (GLA_DK ** -0.5)
    k = k.reshape(bsz, L, GLA_HEADS, GLA_DK)
    v = v.reshape(bsz, L, GLA_HEADS, GLA_DV)
    g_log = g_log.reshape(bsz, L, GLA_HEADS, GLA_DK)
    o = gla_chunked(q, k, v, g_log)
    o = rmsnorm(o, gla_norm_g.reshape(GLA_HEADS, GLA_DV)).astype(x.dtype)
    o = o.reshape(bsz, L, GLA_V) * jax.nn.silu(r)
    y_gla = o @ w_o_gla

    y_c = cb * causal_dwconv(cc * ch, conv_w)
    y_c = y_c * jax.nn.silu(cz)
    y_conv = y_c @ w_o_conv

    merged = jax.nn.sigmoid(ga) * y_gla + jax.nn.sigmoid(gb) * y_conv
    return x + merged @ w_out


def setup_inputs(seed: int = 0) -> dict:
    key = jax.random.key(seed)
    ks = jax.random.split(key, 13)
    f32 = jnp.float32
    return {
        "x": jax.random.normal(ks[0], (BATCH, SEQ, D_MODEL), f32),
        "meta": jax.random.normal(ks[1], (N_META, D_MODEL), f32),
        "norm_g": 1.0 + 0.02 * jax.random.normal(ks[2], (DEPTH, D_MODEL), f32),
        "w_in": jax.random.normal(ks[3], (DEPTH, D_MODEL, N_IN), f32) * D_MODEL ** -0.5,
        "w_gate_up": jax.random.normal(ks[4], (DEPTH, GATE_RANK, GLA_K), f32) * GATE_RANK ** -0.5,
        "b_gate": 0.1 * jax.random.normal(ks[5], (DEPTH, GLA_K), f32),
        "gla_norm_g": 1.0 + 0.02 * jax.random.normal(ks[6], (DEPTH, GLA_V), f32),
        "w_o_gla": jax.random.normal(ks[7], (DEPTH, GLA_V, D_MODEL), f32) * GLA_V ** -0.5,
        "conv_w": jax.random.normal(ks[8], (DEPTH, CONV_K, CONV_DIM), f32) * CONV_K ** -0.5,
        "w_o_conv": jax.random.normal(ks[9], (DEPTH, CONV_DIM, D_MODEL), f32) * CONV_DIM ** -0.5,
        "w_out": jax.random.normal(ks[10], (DEPTH, D_MODEL, D_MODEL), f32) * D_MODEL ** -0.5,
        "final_norm_g": 1.0 + 0.02 * jax.random.normal(ks[11], (D_MODEL,), f32),
    }


def reference(x, meta, norm_g, w_in, w_gate_up, b_gate, gla_norm_g, w_o_gla, conv_w, w_o_conv, w_out, final_norm_g):
    bsz = x.shape[0]
    meta_b = jnp.broadcast_to(meta.astype(x.dtype)[None], (bsz, N_META, D_MODEL))
    h = jnp.concatenate([meta_b, x], axis=1)
    for l in range(DEPTH):
        h = hybrid_layer(h, norm_g[l], w_in[l], w_gate_up[l], b_gate[l], gla_norm_g[l],
                         w_o_gla[l], conv_w[l], w_o_conv[l], w_out[l])
    return rmsnorm(h, final_norm_g)[:, N_META:]
```

```python
import functools

import jax
import jax.numpy as jnp
from jax import lax
from jax.experimental import pallas as pl
from jax.experimental.pallas import tpu as pltpu

D_MODEL = 1024
N_META = 16
GLA_HEADS = 4
GLA_K = D_MODEL // 2
GLA_V = D_MODEL
GLA_DK = GLA_K // GLA_HEADS
GLA_DV = GLA_V // GLA_HEADS
GATE_RANK = 16
GATE_TAU = 16.0
CHUNK = 64
CONV_K = 3
EPS = 1e-6

_SEC = {name: (i * D_MODEL, (i + 1) * D_MODEL) for i, name in enumerate(
    ("qk", "v", "r", "ch", "cb", "cc", "cz", "ga", "gb"))}
N_MAIN = 9 * D_MODEL

VMEM_LIMIT_BYTES = 60 * 1024 * 1024
TOKEN_TILE = 256


def _sigmoid(z):
    return 1.0 / (1.0 + jnp.exp(-z))


def _silu(z):
    return z * _sigmoid(z)


def _log_sigmoid(z):
    return jnp.minimum(z, 0.0) - jnp.log(1.0 + jnp.exp(-jnp.abs(z)))


def _bf16(a):
    return a.astype(jnp.bfloat16)


def _dot(a, b):
    return jnp.dot(a, b, preferred_element_type=jnp.float32)


def _split3_bf16(a):
    hi = _bf16(a)
    r1 = a - hi.astype(jnp.float32)
    mid = _bf16(r1)
    lo = _bf16(r1 - mid.astype(jnp.float32))
    return hi, mid, lo


def _layer_kernel(x_ref, s0_ref, tail0_ref, norm_g_ref, w_main_ref, w_glr_ref, w_gate_up_ref,
                  b_gate_ref, gla_norm_g_ref, w_o_gla_ref, conv_w_ref, w_o_conv_ref, w_out_ref,
                  final_g_ref,
                  y_ref, s_out_ref, tail_out_ref,
                  state_ref, tail_ref, q_ref, k_ref, v_ref, g_ref, o_ref,
                  *, tile, chunk, apply_final_norm):
    t = pl.program_id(1)

    @pl.when(t == 0)
    def _():
        state_ref[...] = s0_ref[...]
        tail_ref[...] = tail0_ref[...]

    x = x_ref[...]
    h = x * lax.rsqrt(jnp.mean(x * x, axis=-1, keepdims=True) + EPS) * norm_g_ref[...]
    h = _bf16(h)

    def proj(name):
        lo, hi = _SEC[name]
        return _dot(h, w_main_ref[:, lo:hi])

    qk = proj("qk")
    q_ref[...] = qk[:, :GLA_K]
    k_ref[...] = qk[:, GLA_K:]
    v_ref[...] = _bf16(proj("v"))
    glr = _dot(h, w_glr_ref[...])
    z = _dot(_bf16(glr), w_gate_up_ref[...]) + b_gate_ref[...]
    g_ref[...] = _log_sigmoid(z) * (1.0 / GATE_TAU)

    row = lax.broadcasted_iota(jnp.int32, (chunk, chunk), 0)
    col = lax.broadcasted_iota(jnp.int32, (chunk, chunk), 1)
    causal = row >= col
    tril = _bf16(causal.astype(jnp.float32))

    def chunk_body(c, carry):
        rows = pl.ds(pl.multiple_of(c * chunk, chunk), chunk)
        g = g_ref[rows, :]
        parts = _split3_bf16(g)
        b = _dot(tril, parts[0]) + _dot(tril, parts[1]) + _dot(tril, parts[2])
        b_last = b[chunk - 1:chunk, :]
        q_in = _bf16(q_ref[rows, :] * (jnp.exp(b) * (GLA_DK ** -0.5)))
        kc = k_ref[rows, :]
        k_in = _bf16(kc * jnp.exp(-b))
        k_st = _bf16(kc * jnp.exp(b_last - b))
        decay = jnp.exp(b_last)
        vc = v_ref[rows, :]
        for hd in range(GLA_HEADS):
            ks = slice(hd * GLA_DK, (hd + 1) * GLA_DK)
            vs = slice(hd * GLA_DV, (hd + 1) * GLA_DV)
            att = lax.dot_general(q_in[:, ks], k_in[:, ks], (((1,), (1,)), ((), ())),
                                  preferred_element_type=jnp.float32)
            att = _bf16(jnp.where(causal, att, 0.0))
            st = state_ref[hd]
            o_inter = lax.dot_general(q_in[:, ks], _bf16(st), (((1,), (1,)), ((), ())),
                                      preferred_element_type=jnp.float32)
            o_ref[rows, vs] = _dot(att, vc[:, vs]) + o_inter
            kv = lax.dot_general(vc[:, vs], k_st[:, ks], (((0,), (0,)), ((), ())),
                                 preferred_element_type=jnp.float32)
            state_ref[hd] = st * decay[:, ks] + kv
        return carry

    lax.fori_loop(0, tile // chunk, chunk_body, 0)

    r = proj("r")
    gng = gla_norm_g_ref[...]
    o_heads = []
    for hd in range(GLA_HEADS):
        vs = slice(hd * GLA_DV, (hd + 1) * GLA_DV)
        oh = o_ref[:, vs]
        oh = oh * lax.rsqrt(jnp.mean(oh * oh, axis=-1, keepdims=True) + EPS) * gng[:, vs]
        o_heads.append(_bf16(oh * _silu(r[:, vs])))
    y_gla = _dot(jnp.concatenate(o_heads, axis=-1), w_o_gla_ref[...])

    u = proj("cc") * proj("ch")
    tail = tail_ref[...]
    tok = lax.broadcasted_iota(jnp.int32, (tile, D_MODEL), 0)
    u1 = jnp.where(tok == 0, tail[1:2, :], pltpu.roll(u, 1, 0))
    u2 = jnp.where(tok == 0, tail[0:1, :],
                   jnp.where(tok == 1, tail[1:2, :], pltpu.roll(u, 2, 0)))
    tail_ref[...] = u[tile - 2:tile, :]
    cw = conv_w_ref[...]
    y_c = cw[0:1, :] * u2 + cw[1:2, :] * u1 + cw[2:3, :] * u
    y_c = proj("cb") * y_c * _silu(proj("cz"))
    y_conv = _dot(_bf16(y_c), w_o_conv_ref[...])

    merged = _sigmoid(proj("ga")) * y_gla + _sigmoid(proj("gb")) * y_conv
    y = x + _dot(_bf16(merged), w_out_ref[...])
    if apply_final_norm:
        y = y * lax.rsqrt(jnp.mean(y * y, axis=-1, keepdims=True) + EPS) * final_g_ref[...]
    y_ref[...] = y

    @pl.when(t == pl.num_programs(1) - 1)
    def _():
        s_out_ref[...] = state_ref[...]
        tail_out_ref[...] = tail_ref[...]


def _resident(shape):
    return pl.BlockSpec(shape, lambda b, t: (0,) * len(shape), pipeline_mode=pl.Buffered(1))


def _layer_call(x, s0, tail0, wts, final_g, *, tile, chunk, apply_final_norm):
    bsz, seq, _ = x.shape
    assert seq % tile == 0 and tile % chunk == 0
    kern = functools.partial(_layer_kernel, tile=tile, chunk=chunk,
                             apply_final_norm=apply_final_norm)
    tok_spec = pl.BlockSpec((None, tile, D_MODEL), lambda b, t: (b, t, 0))
    state_shape = (GLA_HEADS, GLA_DV, GLA_DK)
    return pl.pallas_call(
        kern,
        grid=(bsz, seq // tile),
        in_specs=[
            tok_spec,
            _resident(state_shape),
            _resident((CONV_K - 1, D_MODEL)),
            _resident((1, D_MODEL)),
            _resident((D_MODEL, N_MAIN)),
            _resident((D_MODEL, GATE_RANK)),
            _resident((GATE_RANK, GLA_K)),
            _resident((1, GLA_K)),
            _resident((1, GLA_V)),
            _resident((GLA_V, D_MODEL)),
            _resident((CONV_K, D_MODEL)),
            _resident((D_MODEL, D_MODEL)),
            _resident((D_MODEL, D_MODEL)),
            _resident((1, D_MODEL)),
        ],
        out_specs=[
            tok_spec,
            pl.BlockSpec(state_shape, lambda b, t: (0, 0, 0)),
            pl.BlockSpec((CONV_K - 1, D_MODEL), lambda b, t: (0, 0)),
        ],
        out_shape=[
            jax.ShapeDtypeStruct(x.shape, jnp.float32),
            jax.ShapeDtypeStruct(state_shape, jnp.float32),
            jax.ShapeDtypeStruct((CONV_K - 1, D_MODEL), jnp.float32),
        ],
        scratch_shapes=[
            pltpu.VMEM(state_shape, jnp.float32),
            pltpu.VMEM((CONV_K - 1, D_MODEL), jnp.float32),
            pltpu.VMEM((tile, GLA_K), jnp.float32),
            pltpu.VMEM((tile, GLA_K), jnp.float32),
            pltpu.VMEM((tile, GLA_V), jnp.bfloat16),
            pltpu.VMEM((tile, GLA_K), jnp.float32),
            pltpu.VMEM((tile, GLA_V), jnp.float32),
        ],
        compiler_params=pltpu.CompilerParams(
            dimension_semantics=("arbitrary", "arbitrary"),
            vmem_limit_bytes=VMEM_LIMIT_BYTES),
        name="hybrid_layer",
    )(x, s0, tail0, wts["norm_g"], wts["w_main"], wts["w_glr"], wts["w_gate_up"], wts["b_gate"],
      wts["gla_norm_g"], wts["w_o_gla"], wts["conv_w"], wts["w_o_conv"], wts["w_out"], final_g)


def _pack_layer(norm_g, w_in, w_gate_up, b_gate, gla_norm_g, w_o_gla, conv_w, w_o_conv, w_out):
    glr_lo = 2 * GLA_K + 2 * GLA_V
    glr_hi = glr_lo + GATE_RANK
    w_main = jnp.concatenate([w_in[:, :glr_lo], w_in[:, glr_hi:]], axis=1)
    return {
        "norm_g": norm_g.reshape(1, D_MODEL),
        "w_main": _bf16(w_main),
        "w_glr": _bf16(w_in[:, glr_lo:glr_hi]),
        "w_gate_up": _bf16(w_gate_up),
        "b_gate": b_gate.reshape(1, GLA_K),
        "gla_norm_g": gla_norm_g.reshape(1, GLA_V),
        "w_o_gla": _bf16(w_o_gla),
        "conv_w": conv_w,
        "w_o_conv": _bf16(w_o_conv),
        "w_out": _bf16(w_out),
    }


def kernel(x, meta, norm_g, w_in, w_gate_up, b_gate, gla_norm_g, w_o_gla, conv_w, w_o_conv, w_out,
           final_norm_g):
    depth = w_in.shape[0]
    final_g = final_norm_g.reshape(1, D_MODEL)
    hm = meta.astype(x.dtype)[None]
    h = x
    zero_state = jnp.zeros((GLA_HEADS, GLA_DV, GLA_DK), jnp.float32)
    zero_tail = jnp.zeros((CONV_K - 1, D_MODEL), jnp.float32)
    for l in range(depth):
        wts = _pack_layer(norm_g[l], w_in[l], w_gate_up[l], b_gate[l], gla_norm_g[l], w_o_gla[l],
                          conv_w[l], w_o_conv[l], w_out[l])
        hm, s_meta, tail_meta = _layer_call(hm, zero_state, zero_tail, wts, final_g,
                                            tile=N_META, chunk=N_META, apply_final_norm=False)
        h, _, _ = _layer_call(h, s_meta, tail_meta, wts, final_g,
                              tile=TOKEN_TILE, chunk=CHUNK, apply_final_norm=(l == depth - 1))
    return h
```

```python
import functools

import jax
import jax.numpy as jnp
from jax import lax
from jax.experimental import pallas as pl
from jax.experimental.pallas import tpu as pltpu

D_MODEL = 1024
N_META = 16
GLA_HEADS = 4
GLA_K = D_MODEL // 2
GLA_V = D_MODEL
GLA_DK = GLA_K // GLA_HEADS
GLA_DV = GLA_V // GLA_HEADS
GATE_RANK = 16
GATE_TAU = 16.0
CHUNK = 64
CONV_K = 3
EPS = 1e-6

_SEC = {name: (i * D_MODEL, (i + 1) * D_MODEL) for i, name in enumerate(
    ("qk", "v", "r", "ch", "cb", "cc", "cz", "ga", "gb"))}
N_MAIN = 9 * D_MODEL

VMEM_LIMIT_BYTES = 60 * 1024 * 1024
TOKEN_TILE = 256


def _sigmoid(z):
    return 1.0 / (1.0 + jnp.exp(-z))


def _silu(z):
    return z * _sigmoid(z)


def _log_sigmoid(z):
    return jnp.minimum(z, 0.0) - jnp.log(1.0 + jnp.exp(-jnp.abs(z)))


def _bf16(a):
    return a.astype(jnp.bfloat16)


def _dot(a, b):
    return jnp.dot(a, b, preferred_element_type=jnp.float32)


def _split3_bf16(a):
    hi = _bf16(a)
    r1 = a - hi.astype(jnp.float32)
    mid = _bf16(r1)
    lo = _bf16(r1 - mid.astype(jnp.float32))
    return hi, mid, lo


def _layer_kernel(x_ref, s0_ref, tail0_ref, norm_g_ref, w_main_ref, w_glr_ref, w_gate_up_ref,
                  b_gate_ref, gla_norm_g_ref, w_o_gla_ref, conv_w_ref, w_o_conv_ref, w_out_ref,
                  final_g_ref,
                  y_ref, s_out_ref, tail_out_ref,
                  state_ref, tail_ref,
                  *, tile, chunk, apply_final_norm):
    t = pl.program_id(1)

    @pl.when(t == 0)
    def _():
        state_ref[...] = s0_ref[...]
        tail_ref[...] = tail0_ref[...]

    x = x_ref[...]
    h = x * lax.rsqrt(jnp.mean(x * x, axis=-1, keepdims=True) + EPS) * norm_g_ref[...]
    h = _bf16(h)

    def proj(name):
        lo, hi = _SEC[name]
        return _dot(h, w_main_ref[:, lo:hi])

    n_chunks = tile // chunk
    qk = proj("qk")
    q = qk[:, :GLA_K]
    k = qk[:, GLA_K:]
    v = _bf16(proj("v"))
    glr = _dot(h, w_glr_ref[...])
    z = _dot(_bf16(glr), w_gate_up_ref[...]) + b_gate_ref[...]
    g = _log_sigmoid(z) * (1.0 / GATE_TAU)

    ti = lax.broadcasted_iota(jnp.int32, (tile, tile), 0)
    si = lax.broadcasted_iota(jnp.int32, (tile, tile), 1)
    causal = ti >= si
    shift = chunk.bit_length() - 1
    same_chunk = lax.shift_right_logical(ti, shift) == lax.shift_right_logical(si, shift)
    tril = _bf16((causal & same_chunk).astype(jnp.float32))
    parts = _split3_bf16(g)
    b = _dot(tril, parts[0]) + _dot(tril, parts[1]) + _dot(tril, parts[2])

    def rows(c):
        return slice(c * chunk, (c + 1) * chunk)

    b_last = [b[(c + 1) * chunk - 1:(c + 1) * chunk, :] for c in range(n_chunks)]
    base = [jnp.zeros((1, GLA_K), jnp.float32)]
    for c in range(n_chunks):
        base.append(base[c] + b_last[c])

    q_in = q * (jnp.exp(b) * (GLA_DK ** -0.5))
    k_in = k * jnp.exp(-b)
    q_in_c, q_dec_c, k_in_c, k_st_c, k_end_c = [], [], [], [], []
    for c in range(n_chunks):
        q_c = q_in[rows(c), :]
        q_in_c.append(_bf16(q_c))
        q_dec_c.append(_bf16(q_c * jnp.exp(base[c])))
        k_in_c.append(_bf16(k_in[rows(c), :]))
        k_st = k[rows(c), :] * jnp.exp(b_last[c] - b[rows(c), :])
        k_st_c.append(k_st)
        k_end_c.append(_bf16(k_st * jnp.exp(base[n_chunks] - base[c + 1])))
    q_dec = jnp.concatenate(q_dec_c, axis=0)
    k_end = jnp.concatenate(k_end_c, axis=0)
    tile_decay = jnp.exp(base[n_chunks])

    zero_rows = jnp.zeros((chunk, GLA_K), jnp.bfloat16)
    k_seen = []
    for c in range(n_chunks):
        blocks = [_bf16(k_st_c[j] * jnp.exp(base[c] - base[j + 1])) for j in range(c)]
        blocks.append(k_in_c[c])
        blocks.extend([zero_rows] * (n_chunks - 1 - c))
        k_seen.append(jnp.concatenate(blocks, axis=0))

    nt_dims = (((1,), (1,)), ((), ()))
    r = proj("r")
    gng = gla_norm_g_ref[...]
    o_heads = []
    for hd in range(GLA_HEADS):
        ks = slice(hd * GLA_DK, (hd + 1) * GLA_DK)
        vs = slice(hd * GLA_DV, (hd + 1) * GLA_DV)
        att = jnp.concatenate(
            [lax.dot_general(q_in_c[c][:, ks], k_seen[c][:, ks], nt_dims,
                             preferred_element_type=jnp.float32) for c in range(n_chunks)],
            axis=0)
        att = _bf16(jnp.where(causal, att, 0.0))
        st = state_ref[hd]
        oh = _dot(att, v[:, vs]) + lax.dot_general(q_dec[:, ks], _bf16(st), nt_dims,
                                                   preferred_element_type=jnp.float32)
        kv = lax.dot_general(v[:, vs], k_end[:, ks], (((0,), (0,)), ((), ())),
                             preferred_element_type=jnp.float32)
        state_ref[hd] = st * tile_decay[:, ks] + kv
        oh = oh * lax.rsqrt(jnp.mean(oh * oh, axis=-1, keepdims=True) + EPS) * gng[:, vs]
        o_heads.append(_bf16(oh * _silu(r[:, vs])))
    y_gla = _dot(jnp.concatenate(o_heads, axis=-1), w_o_gla_ref[...])

    u = proj("cc") * proj("ch")
    tail = tail_ref[...]
    tok = lax.broadcasted_iota(jnp.int32, (tile, D_MODEL), 0)
    u1 = jnp.where(tok == 0, tail[1:2, :], pltpu.roll(u, 1, 0))
    u2 = jnp.where(tok == 0, tail[0:1, :],
                   jnp.where(tok == 1, tail[1:2, :], pltpu.roll(u, 2, 0)))
    tail_ref[...] = u[tile - 2:tile, :]
    cw = conv_w_ref[...]
    y_c = cw[0:1, :] * u2 + cw[1:2, :] * u1 + cw[2:3, :] * u
    y_c = proj("cb") * y_c * _silu(proj("cz"))
    y_conv = _dot(_bf16(y_c), w_o_conv_ref[...])

    merged = _sigmoid(proj("ga")) * y_gla + _sigmoid(proj("gb")) * y_conv
    y = x + _dot(_bf16(merged), w_out_ref[...])
    if apply_final_norm:
        y = y * lax.rsqrt(jnp.mean(y * y, axis=-1, keepdims=True) + EPS) * final_g_ref[...]
    y_ref[...] = y

    @pl.when(t == pl.num_programs(1) - 1)
    def _():
        s_out_ref[...] = state_ref[...]
        tail_out_ref[...] = tail_ref[...]


def _resident(shape):
    return pl.BlockSpec(shape, lambda b, t: (0,) * len(shape), pipeline_mode=pl.Buffered(1))


def _layer_call(x, s0, tail0, wts, final_g, *, tile, chunk, apply_final_norm):
    bsz, seq, _ = x.shape
    assert seq % tile == 0 and tile % chunk == 0
    kern = functools.partial(_layer_kernel, tile=tile, chunk=chunk,
                             apply_final_norm=apply_final_norm)
    tok_spec = pl.BlockSpec((None, tile, D_MODEL), lambda b, t: (b, t, 0))
    state_shape = (GLA_HEADS, GLA_DV, GLA_DK)
    return pl.pallas_call(
        kern,
        grid=(bsz, seq // tile),
        in_specs=[
            tok_spec,
            _resident(state_shape),
            _resident((CONV_K - 1, D_MODEL)),
            _resident((1, D_MODEL)),
            _resident((D_MODEL, N_MAIN)),
            _resident((D_MODEL, GATE_RANK)),
            _resident((GATE_RANK, GLA_K)),
            _resident((1, GLA_K)),
            _resident((1, GLA_V)),
            _resident((GLA_V, D_MODEL)),
            _resident((CONV_K, D_MODEL)),
            _resident((D_MODEL, D_MODEL)),
            _resident((D_MODEL, D_MODEL)),
            _resident((1, D_MODEL)),
        ],
        out_specs=[
            tok_spec,
            pl.BlockSpec(state_shape, lambda b, t: (0, 0, 0)),
            pl.BlockSpec((CONV_K - 1, D_MODEL), lambda b, t: (0, 0)),
        ],
        out_shape=[
            jax.ShapeDtypeStruct(x.shape, jnp.float32),
            jax.ShapeDtypeStruct(state_shape, jnp.float32),
            jax.ShapeDtypeStruct((CONV_K - 1, D_MODEL), jnp.float32),
        ],
        scratch_shapes=[
            pltpu.VMEM(state_shape, jnp.float32),
            pltpu.VMEM((CONV_K - 1, D_MODEL), jnp.float32),
        ],
        compiler_params=pltpu.CompilerParams(
            dimension_semantics=("arbitrary", "arbitrary"),
            vmem_limit_bytes=VMEM_LIMIT_BYTES),
        name="hybrid_layer",
    )(x, s0, tail0, wts["norm_g"], wts["w_main"], wts["w_glr"], wts["w_gate_up"], wts["b_gate"],
      wts["gla_norm_g"], wts["w_o_gla"], wts["conv_w"], wts["w_o_conv"], wts["w_out"], final_g)


def _pack_layer(norm_g, w_in, w_gate_up, b_gate, gla_norm_g, w_o_gla, conv_w, w_o_conv, w_out):
    glr_lo = 2 * GLA_K + 2 * GLA_V
    glr_hi = glr_lo + GATE_RANK
    w_main = jnp.concatenate([w_in[:, :glr_lo], w_in[:, glr_hi:]], axis=1)
    return {
        "norm_g": norm_g.reshape(1, D_MODEL),
        "w_main": _bf16(w_main),
        "w_glr": _bf16(w_in[:, glr_lo:glr_hi]),
        "w_gate_up": _bf16(w_gate_up),
        "b_gate": b_gate.reshape(1, GLA_K),
        "gla_norm_g": gla_norm_g.reshape(1, GLA_V),
        "w_o_gla": _bf16(w_o_gla),
        "conv_w": conv_w,
        "w_o_conv": _bf16(w_o_conv),
        "w_out": _bf16(w_out),
    }


def kernel(x, meta, norm_g, w_in, w_gate_up, b_gate, gla_norm_g, w_o_gla, conv_w, w_o_conv, w_out,
           final_norm_g):
    depth = w_in.shape[0]
    final_g = final_norm_g.reshape(1, D_MODEL)
    hm = meta.astype(x.dtype)[None]
    h = x
    zero_state = jnp.zeros((GLA_HEADS, GLA_DV, GLA_DK), jnp.float32)
    zero_tail = jnp.zeros((CONV_K - 1, D_MODEL), jnp.float32)
    for l in range(depth):
        wts = _pack_layer(norm_g[l], w_in[l], w_gate_up[l], b_gate[l], gla_norm_g[l], w_o_gla[l],
                          conv_w[l], w_o_conv[l], w_out[l])
        hm, s_meta, tail_meta = _layer_call(hm, zero_state, zero_tail, wts, final_g,
                                            tile=N_META, chunk=N_META, apply_final_norm=False)
        h, _, _ = _layer_call(h, s_meta, tail_meta, wts, final_g,
                              tile=TOKEN_TILE, chunk=CHUNK, apply_final_norm=(l == depth - 1))
    return h
```

```python
import functools

import jax
import jax.numpy as jnp
from jax import lax
from jax.experimental import pallas as pl
from jax.experimental.pallas import tpu as pltpu

D_MODEL = 1024
N_META = 16
GLA_HEADS = 4
GLA_K = D_MODEL // 2
GLA_V = D_MODEL
GLA_DK = GLA_K // GLA_HEADS
GLA_DV = GLA_V // GLA_HEADS
GATE_RANK = 16
GATE_TAU = 16.0
CHUNK = 64
CONV_K = 3
EPS = 1e-6

_GLA_SECTIONS = ("qk", "v", "r")
_CONV_SECTIONS = ("ch", "cb", "cc", "cz", "ga", "gb")
N_GLA_COLS = len(_GLA_SECTIONS) * D_MODEL
N_CONV_COLS = len(_CONV_SECTIONS) * D_MODEL

VMEM_LIMIT_BYTES = 60 * 1024 * 1024
TOKEN_TILE = 256


def _sigmoid(z):
    return 1.0 / (1.0 + jnp.exp(-z))


def _silu(z):
    return z * _sigmoid(z)


def _log_sigmoid(z):
    return jnp.minimum(z, 0.0) - jnp.log(1.0 + jnp.exp(-jnp.abs(z)))


def _bf16(a):
    return a.astype(jnp.bfloat16)


def _dot(a, b):
    return jnp.dot(a, b, preferred_element_type=jnp.float32)


def _split3_bf16(a):
    hi = _bf16(a)
    r1 = a - hi.astype(jnp.float32)
    mid = _bf16(r1)
    lo = _bf16(r1 - mid.astype(jnp.float32))
    return hi, mid, lo


def _layer_kernel(x_ref, s0_ref, tail0_ref, norm_g_ref, w_gla_ref, w_conv_ref, w_glr_ref,
                  w_gate_up_ref, b_gate_ref, gla_norm_g_ref, w_o_gla_ref, conv_w_ref, w_o_conv_ref, w_out_ref,
                  final_g_ref,
                  y_ref, s_out_ref, tail_out_ref,
                  state_ref, tail_ref,
                  *, tile, chunk, apply_final_norm):
    t = pl.program_id(1)

    @pl.when(t == 0)
    def _():
        state_ref[...] = s0_ref[...]
        tail_ref[...] = tail0_ref[...]

    x = x_ref[...]
    h = x * lax.rsqrt(jnp.mean(x * x, axis=-1, keepdims=True) + EPS) * norm_g_ref[...]
    h = _bf16(h)

    def proj(name):
        if name in _GLA_SECTIONS:
            w_ref, i = w_gla_ref, _GLA_SECTIONS.index(name)
        else:
            w_ref, i = w_conv_ref, _CONV_SECTIONS.index(name)
        return _dot(h, w_ref[:, i * D_MODEL:(i + 1) * D_MODEL])

    n_chunks = tile // chunk
    qk = proj("qk")
    q = qk[:, :GLA_K]
    k = qk[:, GLA_K:]
    v = _bf16(proj("v"))
    glr = _dot(h, w_glr_ref[...])
    z = _dot(_bf16(glr), w_gate_up_ref[...]) + b_gate_ref[...]
    g = _log_sigmoid(z) * (1.0 / GATE_TAU)

    ti = lax.broadcasted_iota(jnp.int32, (tile, tile), 0)
    si = lax.broadcasted_iota(jnp.int32, (tile, tile), 1)
    causal = ti >= si
    shift = chunk.bit_length() - 1
    same_chunk = lax.shift_right_logical(ti, shift) == lax.shift_right_logical(si, shift)
    tril = _bf16((causal & same_chunk).astype(jnp.float32))
    parts = _split3_bf16(g)
    b = _dot(tril, parts[0]) + _dot(tril, parts[1]) + _dot(tril, parts[2])

    def rows(c):
        return slice(c * chunk, (c + 1) * chunk)

    b_last = [b[(c + 1) * chunk - 1:(c + 1) * chunk, :] for c in range(n_chunks)]
    base = [jnp.zeros((1, GLA_K), jnp.float32)]
    for c in range(n_chunks):
        base.append(base[c] + b_last[c])

    q_in = q * (jnp.exp(b) * (GLA_DK ** -0.5))
    k_in = k * jnp.exp(-b)
    q_in_c, q_dec_c, k_in_c, k_st_c, k_end_c = [], [], [], [], []
    for c in range(n_chunks):
        q_c = q_in[rows(c), :]
        q_in_c.append(_bf16(q_c))
        q_dec_c.append(_bf16(q_c * jnp.exp(base[c])))
        k_in_c.append(_bf16(k_in[rows(c), :]))
        k_st = k[rows(c), :] * jnp.exp(b_last[c] - b[rows(c), :])
        k_st_c.append(k_st)
        k_end_c.append(_bf16(k_st * jnp.exp(base[n_chunks] - base[c + 1])))
    q_dec = jnp.concatenate(q_dec_c, axis=0)
    k_end = jnp.concatenate(k_end_c, axis=0)
    tile_decay = jnp.exp(base[n_chunks])

    zero_rows = jnp.zeros((chunk, GLA_K), jnp.bfloat16)
    k_seen = []
    for c in range(n_chunks):
        blocks = [_bf16(k_st_c[j] * jnp.exp(base[c] - base[j + 1])) for j in range(c)]
        blocks.append(k_in_c[c])
        blocks.extend([zero_rows] * (n_chunks - 1 - c))
        k_seen.append(jnp.concatenate(blocks, axis=0))

    nt_dims = (((1,), (1,)), ((), ()))
    r = proj("r")
    gng = gla_norm_g_ref[...]
    o_heads = []
    for hd in range(GLA_HEADS):
        ks = slice(hd * GLA_DK, (hd + 1) * GLA_DK)
        vs = slice(hd * GLA_DV, (hd + 1) * GLA_DV)
        att = jnp.concatenate(
            [lax.dot_general(q_in_c[c][:, ks], k_seen[c][:, ks], nt_dims,
                             preferred_element_type=jnp.float32) for c in range(n_chunks)],
            axis=0)
        att = _bf16(jnp.where(causal, att, 0.0))
        st = state_ref[hd]
        oh = _dot(att, v[:, vs]) + lax.dot_general(q_dec[:, ks], _bf16(st), nt_dims,
                                                   preferred_element_type=jnp.float32)
        kv = lax.dot_general(v[:, vs], k_end[:, ks], (((0,), (0,)), ((), ())),
                             preferred_element_type=jnp.float32)
        state_ref[hd] = st * tile_decay[:, ks] + kv
        oh = oh * lax.rsqrt(jnp.mean(oh * oh, axis=-1, keepdims=True) + EPS) * gng[:, vs]
        o_heads.append(_bf16(oh * _silu(r[:, vs])))
    y_gla = _dot(jnp.concatenate(o_heads, axis=-1), w_o_gla_ref[...])

    u = proj("cc") * proj("ch")
    tail = tail_ref[...]
    tok = lax.broadcasted_iota(jnp.int32, (tile, D_MODEL), 0)
    u1 = jnp.where(tok == 0, tail[1:2, :], pltpu.roll(u, 1, 0))
    u2 = jnp.where(tok == 0, tail[0:1, :],
                   jnp.where(tok == 1, tail[1:2, :], pltpu.roll(u, 2, 0)))
    tail_ref[...] = u[tile - 2:tile, :]
    cw = conv_w_ref[...]
    y_c = cw[0:1, :] * u2 + cw[1:2, :] * u1 + cw[2:3, :] * u
    y_c = proj("cb") * y_c * _silu(proj("cz"))
    y_conv = _dot(_bf16(y_c), w_o_conv_ref[...])

    merged = _sigmoid(proj("ga")) * y_gla + _sigmoid(proj("gb")) * y_conv
    y = x + _dot(_bf16(merged), w_out_ref[...])
    if apply_final_norm:
        y = y * lax.rsqrt(jnp.mean(y * y, axis=-1, keepdims=True) + EPS) * final_g_ref[...]
    y_ref[...] = y

    @pl.when(t == pl.num_programs(1) - 1)
    def _():
        s_out_ref[...] = state_ref[...]
        tail_out_ref[...] = tail_ref[...]


def _resident(shape):
    return pl.BlockSpec(shape, lambda b, t: (0,) * len(shape), pipeline_mode=pl.Buffered(1))


def _layer_slice(layer, shape):
    return pl.BlockSpec((None,) + shape, lambda b, t: (layer,) + (0,) * len(shape),
                        pipeline_mode=pl.Buffered(1))


def _layer_call(x, s0, tail0, params, final_g, layer, *, tile, chunk, apply_final_norm):
    bsz, seq, _ = x.shape
    assert seq % tile == 0 and tile % chunk == 0
    kern = functools.partial(_layer_kernel, tile=tile, chunk=chunk,
                             apply_final_norm=apply_final_norm)
    tok_spec = pl.BlockSpec((None, tile, D_MODEL), lambda b, t: (b, t, 0))
    state_shape = (GLA_HEADS, GLA_DV, GLA_DK)
    return pl.pallas_call(
        kern,
        grid=(bsz, seq // tile),
        in_specs=[
            tok_spec,
            _resident(state_shape),
            _resident((CONV_K - 1, D_MODEL)),
            _layer_slice(layer, (1, D_MODEL)),
            _layer_slice(layer, (D_MODEL, N_GLA_COLS)),
            _layer_slice(layer, (D_MODEL, N_CONV_COLS)),
            _layer_slice(layer, (D_MODEL, GATE_RANK)),
            _layer_slice(layer, (GATE_RANK, GLA_K)),
            _layer_slice(layer, (1, GLA_K)),
            _layer_slice(layer, (1, GLA_V)),
            _layer_slice(layer, (GLA_V, D_MODEL)),
            _layer_slice(layer, (CONV_K, D_MODEL)),
            _layer_slice(layer, (D_MODEL, D_MODEL)),
            _layer_slice(layer, (D_MODEL, D_MODEL)),
            _resident((1, D_MODEL)),
        ],
        out_specs=[
            tok_spec,
            pl.BlockSpec(state_shape, lambda b, t: (0, 0, 0)),
            pl.BlockSpec((CONV_K - 1, D_MODEL), lambda b, t: (0, 0)),
        ],
        out_shape=[
            jax.ShapeDtypeStruct(x.shape, jnp.float32),
            jax.ShapeDtypeStruct(state_shape, jnp.float32),
            jax.ShapeDtypeStruct((CONV_K - 1, D_MODEL), jnp.float32),
        ],
        scratch_shapes=[
            pltpu.VMEM(state_shape, jnp.float32),
            pltpu.VMEM((CONV_K - 1, D_MODEL), jnp.float32),
        ],
        compiler_params=pltpu.CompilerParams(
            dimension_semantics=("arbitrary", "arbitrary"),
            vmem_limit_bytes=VMEM_LIMIT_BYTES),
        name="hybrid_layer",
    )(x, s0, tail0, *params, final_g)


def _pack_params(norm_g, w_in, w_gate_up, b_gate, gla_norm_g, w_o_gla, conv_w, w_o_conv, w_out):
    depth = w_in.shape[0]
    glr_hi = N_GLA_COLS + GATE_RANK
    return (
        norm_g.reshape(depth, 1, D_MODEL),
        _bf16(w_in[:, :, :N_GLA_COLS]),
        _bf16(w_in[:, :, glr_hi:]),
        _bf16(w_in[:, :, N_GLA_COLS:glr_hi]),
        _bf16(w_gate_up),
        b_gate.reshape(depth, 1, GLA_K),
        gla_norm_g.reshape(depth, 1, GLA_V),
        _bf16(w_o_gla),
        conv_w,
        _bf16(w_o_conv),
        _bf16(w_out),
    )


def kernel(x, meta, norm_g, w_in, w_gate_up, b_gate, gla_norm_g, w_o_gla, conv_w, w_o_conv, w_out,
           final_norm_g):
    depth = w_in.shape[0]
    params = _pack_params(norm_g, w_in, w_gate_up, b_gate, gla_norm_g, w_o_gla, conv_w, w_o_conv,
                          w_out)
    final_g = final_norm_g.reshape(1, D_MODEL)
    hm = meta.astype(x.dtype)[None]
    h = x
    zero_state = jnp.zeros((GLA_HEADS, GLA_DV, GLA_DK), jnp.float32)
    zero_tail = jnp.zeros((CONV_K - 1, D_MODEL), jnp.float32)
    for l in range(depth):
        hm, s_meta, tail_meta = _layer_call(hm, zero_state, zero_tail, params, final_g, l,
                                            tile=N_META, chunk=N_META, apply_final_norm=False)
        h, _, _ = _layer_call(h, s_meta, tail_meta, params, final_g, l,
                              tile=TOKEN_TILE, chunk=CHUNK, apply_final_norm=(l == depth - 1))
    return h
```

```python
import functools

import jax
import jax.numpy as jnp
from jax import lax
from jax.experimental import pallas as pl
from jax.experimental.pallas import tpu as pltpu

D_MODEL = 1024
N_META = 16
GLA_HEADS = 4
GLA_K = D_MODEL // 2
GLA_V = D_MODEL
GLA_DK = GLA_K // GLA_HEADS
GLA_DV = GLA_V // GLA_HEADS
GATE_RANK = 16
GATE_TAU = 16.0
CHUNK = 64
CONV_K = 3
EPS = 1e-6
LANES = 128

_GLA_SECTIONS = ("qk", "v", "r")
_CONV_SECTIONS = ("ch", "cb", "cc", "cz", "ga", "gb")
N_GLA_COLS = len(_GLA_SECTIONS) * D_MODEL
N_CONV_COLS = len(_CONV_SECTIONS) * D_MODEL

VMEM_LIMIT_BYTES = 60 * 1024 * 1024
TOKEN_TILE = 512
GLA_BLOCK = 256


def _sigmoid(z):
    return 1.0 / (1.0 + jnp.exp(-z))


def _silu(z):
    return z * _sigmoid(z)


def _log_sigmoid(z):
    return jnp.minimum(z, 0.0) - jnp.log(1.0 + jnp.exp(-jnp.abs(z)))


def _bf16(a):
    return a.astype(jnp.bfloat16)


def _dot(a, b):
    return jnp.dot(a, b, preferred_element_type=jnp.float32)


def _split3_bf16(a):
    hi = _bf16(a)
    r1 = a - hi.astype(jnp.float32)
    mid = _bf16(r1)
    lo = _bf16(r1 - mid.astype(jnp.float32))
    return hi, mid, lo


def _gla_masks(block, chunk):
    ti = lax.broadcasted_iota(jnp.int32, (block, block), 0)
    si = lax.broadcasted_iota(jnp.int32, (block, block), 1)
    causal = ti >= si
    shift = chunk.bit_length() - 1
    same_chunk = lax.shift_right_logical(ti, shift) == lax.shift_right_logical(si, shift)
    return causal, _bf16((causal & same_chunk).astype(jnp.float32))


def _gla_prep(q, k, g, tril, chunk):
    block = q.shape[0]
    n_chunks = block // chunk
    parts = _split3_bf16(g)
    b = _dot(tril, parts[0]) + _dot(tril, parts[1]) + _dot(tril, parts[2])

    def rows(c):
        return slice(c * chunk, (c + 1) * chunk)

    b_last = [b[(c + 1) * chunk - 1:(c + 1) * chunk, :] for c in range(n_chunks)]
    base = [jnp.zeros((1, GLA_K), jnp.float32)]
    for c in range(n_chunks):
        base.append(base[c] + b_last[c])

    q_in = q * (jnp.exp(b) * (GLA_DK ** -0.5))
    k_in = k * jnp.exp(-b)
    q_in_c, q_dec_c, k_in_c, k_st_c, k_end_c = [], [], [], [], []
    for c in range(n_chunks):
        q_c = q_in[rows(c), :]
        q_in_c.append(_bf16(q_c))
        q_dec_c.append(_bf16(q_c * jnp.exp(base[c])))
        k_in_c.append(_bf16(k_in[rows(c), :]))
        k_st = k[rows(c), :] * jnp.exp(b_last[c] - b[rows(c), :])
        k_st_c.append(k_st)
        k_end_c.append(_bf16(k_st * jnp.exp(base[n_chunks] - base[c + 1])))
    q_dec = jnp.concatenate(q_dec_c, axis=0)
    k_end = jnp.concatenate(k_end_c, axis=0)
    block_decay = jnp.exp(base[n_chunks])

    zero_rows = jnp.zeros((chunk, GLA_K), jnp.bfloat16)
    k_seen = []
    for c in range(n_chunks):
        blocks = [_bf16(k_st_c[j] * jnp.exp(base[c] - base[j + 1])) for j in range(c)]
        blocks.append(k_in_c[c])
        blocks.extend([zero_rows] * (n_chunks - 1 - c))
        k_seen.append(jnp.concatenate(blocks, axis=0))
    return q_in_c, k_seen, q_dec, k_end, block_decay


def _gla_attend(operands, v, r, causal, gla_norm_g, state_ref):
    q_in_c, k_seen, q_dec, k_end, block_decay = operands
    n_chunks = len(q_in_c)
    nt_dims = (((1,), (1,)), ((), ()))
    o_heads = []
    for hd in range(GLA_HEADS):
        ks = slice(hd * GLA_DK, (hd + 1) * GLA_DK)
        vs = slice(hd * GLA_DV, (hd + 1) * GLA_DV)
        att = jnp.concatenate(
            [lax.dot_general(q_in_c[c][:, ks], k_seen[c][:, ks], nt_dims,
                             preferred_element_type=jnp.float32) for c in range(n_chunks)],
            axis=0)
        att = _bf16(jnp.where(causal, att, 0.0))
        st = state_ref[hd]
        oh = _dot(att, v[:, vs]) + lax.dot_general(q_dec[:, ks], _bf16(st), nt_dims,
                                                   preferred_element_type=jnp.float32)
        kv = lax.dot_general(v[:, vs], k_end[:, ks], (((0,), (0,)), ((), ())),
                             preferred_element_type=jnp.float32)
        state_ref[hd] = st * block_decay[:, ks] + kv
        oh = oh * lax.rsqrt(jnp.mean(oh * oh, axis=-1, keepdims=True) + EPS) * gla_norm_g[:, vs]
        o_heads.append(_bf16(oh * _silu(r[:, vs])))
    return jnp.concatenate(o_heads, axis=-1)


def _layer_kernel(x_ref, s0_ref, tail0_ref, norm_g_ref, w_gla_ref, w_conv_ref, w_glr_ref,
                  w_gate_up_ref, b_gate_ref, gla_norm_g_ref, w_o_gla_ref, conv_w_ref,
                  w_o_conv_ref, w_out_ref, final_g_ref,
                  y_ref, s_out_ref, tail_out_ref,
                  state_ref, tail_ref,
                  *, tile, gla_block, chunk, apply_final_norm):
    t = pl.program_id(1)

    @pl.when(t == 0)
    def _():
        state_ref[...] = s0_ref[...]
        tail_ref[...] = tail0_ref[...]

    x = x_ref[...]
    h = x * lax.rsqrt(jnp.mean(x * x, axis=-1, keepdims=True) + EPS) * norm_g_ref[...]
    h = _bf16(h)

    def proj(name):
        if name in _GLA_SECTIONS:
            w_ref, i = w_gla_ref, _GLA_SECTIONS.index(name)
        else:
            w_ref, i = w_conv_ref, _CONV_SECTIONS.index(name)
        return _dot(h, w_ref[:, i * D_MODEL:(i + 1) * D_MODEL])

    glr = _dot(h, w_glr_ref[:, :GATE_RANK])
    z = _dot(_bf16(glr), w_gate_up_ref[...]) + b_gate_ref[...]
    g = _log_sigmoid(z) * (1.0 / GATE_TAU)
    qk = proj("qk")
    blocks = [slice(s * gla_block, (s + 1) * gla_block) for s in range(tile // gla_block)]
    causal, tril = _gla_masks(gla_block, chunk)
    operands = [_gla_prep(qk[blk, :GLA_K], qk[blk, GLA_K:], g[blk, :], tril, chunk)
                for blk in blocks]
    v = _bf16(proj("v"))
    r = proj("r")
    gla_norm_g = gla_norm_g_ref[...]
    gated = [_gla_attend(operands[0], v[blocks[0], :], r[blocks[0], :], causal, gla_norm_g,
                         state_ref)]
    cc = proj("cc")
    ch = proj("ch")
    cb = proj("cb")
    for s in range(1, len(blocks)):
        gated.append(_gla_attend(operands[s], v[blocks[s], :], r[blocks[s], :], causal,
                                 gla_norm_g, state_ref))
    cz = proj("cz")
    y_gla = _dot(jnp.concatenate(gated, axis=0), w_o_gla_ref[...])

    u = cc * ch
    tail = tail_ref[...]
    tok = lax.broadcasted_iota(jnp.int32, (tile, D_MODEL), 0)
    u1 = jnp.where(tok == 0, tail[1:2, :], pltpu.roll(u, 1, 0))
    u2 = jnp.where(tok == 0, tail[0:1, :],
                   jnp.where(tok == 1, tail[1:2, :], pltpu.roll(u, 2, 0)))
    tail_ref[...] = u[tile - 2:tile, :]
    cw = conv_w_ref[...]
    y_c = cw[0:1, :] * u2 + cw[1:2, :] * u1 + cw[2:3, :] * u
    y_c = cb * y_c * _silu(cz)
    y_conv = _dot(_bf16(y_c), w_o_conv_ref[...])

    merged = _sigmoid(proj("ga")) * y_gla + _sigmoid(proj("gb")) * y_conv
    y = x + _dot(_bf16(merged), w_out_ref[...])
    if apply_final_norm:
        y = y * lax.rsqrt(jnp.mean(y * y, axis=-1, keepdims=True) + EPS) * final_g_ref[...]
    y_ref[...] = y

    @pl.when(t == pl.num_programs(1) - 1)
    def _():
        s_out_ref[...] = state_ref[...]
        tail_out_ref[...] = tail_ref[...]


def _resident(shape):
    return pl.BlockSpec(shape, lambda b, t: (0,) * len(shape), pipeline_mode=pl.Buffered(1))


def _layer_slice(layer, shape):
    return pl.BlockSpec((None,) + shape, lambda b, t: (layer,) + (0,) * len(shape),
                        pipeline_mode=pl.Buffered(1))


def _layer_call(x, s0, tail0, params, final_g, layer, *, tile, chunk, apply_final_norm):
    bsz, seq, _ = x.shape
    gla_block = min(tile, GLA_BLOCK)
    assert seq % tile == 0 and tile % gla_block == 0 and gla_block % chunk == 0
    kern = functools.partial(_layer_kernel, tile=tile, gla_block=gla_block, chunk=chunk,
                             apply_final_norm=apply_final_norm)
    tok_spec = pl.BlockSpec((None, tile, D_MODEL), lambda b, t: (b, t, 0))
    state_shape = (GLA_HEADS, GLA_DV, GLA_DK)
    return pl.pallas_call(
        kern,
        grid=(bsz, seq // tile),
        in_specs=[
            tok_spec,
            _resident(state_shape),
            _resident((CONV_K - 1, D_MODEL)),
            _layer_slice(layer, (1, D_MODEL)),
            _layer_slice(layer, (D_MODEL, N_GLA_COLS)),
            _layer_slice(layer, (D_MODEL, N_CONV_COLS)),
            pl.BlockSpec((None, D_MODEL, LANES), lambda b, t: (layer, 0, N_GLA_COLS // LANES),
                         pipeline_mode=pl.Buffered(1)),
            _layer_slice(layer, (GATE_RANK, GLA_K)),
            _layer_slice(layer, (1, GLA_K)),
            _layer_slice(layer, (1, GLA_V)),
            _layer_slice(layer, (GLA_V, D_MODEL)),
            _layer_slice(layer, (CONV_K, D_MODEL)),
            _layer_slice(layer, (D_MODEL, D_MODEL)),
            _layer_slice(layer, (D_MODEL, D_MODEL)),
            _resident((1, D_MODEL)),
        ],
        out_specs=[
            tok_spec,
            pl.BlockSpec(state_shape, lambda b, t: (0, 0, 0)),
            pl.BlockSpec((CONV_K - 1, D_MODEL), lambda b, t: (0, 0)),
        ],
        out_shape=[
            jax.ShapeDtypeStruct(x.shape, jnp.float32),
            jax.ShapeDtypeStruct(state_shape, jnp.float32),
            jax.ShapeDtypeStruct((CONV_K - 1, D_MODEL), jnp.float32),
        ],
        scratch_shapes=[
            pltpu.VMEM(state_shape, jnp.float32),
            pltpu.VMEM((CONV_K - 1, D_MODEL), jnp.float32),
        ],
        compiler_params=pltpu.CompilerParams(
            dimension_semantics=("arbitrary", "arbitrary"),
            vmem_limit_bytes=VMEM_LIMIT_BYTES),
        name="hybrid_layer",
    )(x, s0, tail0, *params, final_g)


def _pack_params(norm_g, w_in, w_gate_up, b_gate, gla_norm_g, w_o_gla, conv_w, w_o_conv, w_out):
    depth = w_in.shape[0]
    w_in = _bf16(w_in)
    return (
        norm_g.reshape(depth, 1, D_MODEL),
        w_in,
        w_in[:, :, N_GLA_COLS + GATE_RANK:],
        w_in,
        _bf16(w_gate_up),
        b_gate.reshape(depth, 1, GLA_K),
        gla_norm_g.reshape(depth, 1, GLA_V),
        _bf16(w_o_gla),
        conv_w,
        _bf16(w_o_conv),
        _bf16(w_out),
    )


def kernel(x, meta, norm_g, w_in, w_gate_up, b_gate, gla_norm_g, w_o_gla, conv_w, w_o_conv, w_out,
           final_norm_g):
    depth = w_in.shape[0]
    params = _pack_params(norm_g, w_in, w_gate_up, b_gate, gla_norm_g, w_o_gla, conv_w, w_o_conv,
                          w_out)
    final_g = final_norm_g.reshape(1, D_MODEL)
    hm = meta.astype(x.dtype)[None]
    h = x
    zero_state = jnp.zeros((GLA_HEADS, GLA_DV, GLA_DK), jnp.float32)
    zero_tail = jnp.zeros((CONV_K - 1, D_MODEL), jnp.float32)
    for l in range(depth):
        hm, s_meta, tail_meta = _layer_call(hm, zero_state, zero_tail, params, final_g, l,
                                            tile=N_META, chunk=N_META, apply_final_norm=False)
        h, _, _ = _layer_call(h, s_meta, tail_meta, params, final_g, l,
                              tile=TOKEN_TILE, chunk=CHUNK, apply_final_norm=(l == depth - 1))
    return h
```

```python
import functools

import jax
import jax.numpy as jnp
from jax import lax
from jax.experimental import pallas as pl
from jax.experimental.pallas import tpu as pltpu

D_MODEL = 1024
N_META = 16
GLA_HEADS = 4
GLA_K = D_MODEL // 2
GLA_V = D_MODEL
GLA_DK = GLA_K // GLA_HEADS
GLA_DV = GLA_V // GLA_HEADS
GATE_RANK = 16
GATE_TAU = 16.0
CHUNK = 64
CONV_K = 3
EPS = 1e-6
LANES = 128

_GLA_SECTIONS = ("qk", "v", "r")
_CONV_SECTIONS = ("ch", "cb", "cc", "cz", "ga", "gb")
N_GLA_COLS = len(_GLA_SECTIONS) * D_MODEL
N_CONV_COLS = len(_CONV_SECTIONS) * D_MODEL

VMEM_LIMIT_BYTES = 60 * 1024 * 1024
TOKEN_TILE = 512
GLA_BLOCK = 256


def _sigmoid(z):
    return 1.0 / (1.0 + jnp.exp(-z))


def _silu(z):
    return z * _sigmoid(z)


def _log_sigmoid(z):
    return jnp.minimum(z, 0.0) - jnp.log(1.0 + jnp.exp(-jnp.abs(z)))


def _bf16(a):
    return a.astype(jnp.bfloat16)


def _dot(a, b):
    return jnp.dot(a, b, preferred_element_type=jnp.float32)


def _split2_bf16(a):
    hi = _bf16(a)
    lo = _bf16(a - hi.astype(jnp.float32))
    return hi, lo


def _gla_masks(block, chunk):
    ti = lax.broadcasted_iota(jnp.int32, (block, block), 0)
    si = lax.broadcasted_iota(jnp.int32, (block, block), 1)
    causal = ti >= si
    shift = chunk.bit_length() - 1
    same_chunk = lax.shift_right_logical(ti, shift) == lax.shift_right_logical(si, shift)
    return causal, _bf16((causal & same_chunk).astype(jnp.float32))


def _gla_prep(q, k, g, tril, chunk):
    block = q.shape[0]
    n_chunks = block // chunk
    g_hi, g_lo = _split2_bf16(g)
    b = _dot(tril, g_hi) + _dot(tril, g_lo)

    def rows(c):
        return slice(c * chunk, (c + 1) * chunk)

    b_last = [b[(c + 1) * chunk - 1:(c + 1) * chunk, :] for c in range(n_chunks)]
    base = [jnp.zeros((1, GLA_K), jnp.float32)]
    for c in range(n_chunks):
        base.append(base[c] + b_last[c])

    q_in = q * (jnp.exp(b) * (GLA_DK ** -0.5))
    k_in = k * jnp.exp(-b)
    q_in_c, q_dec_c, k_in_c, k_st_c, k_end_c = [], [], [], [], []
    for c in range(n_chunks):
        q_c = q_in[rows(c), :]
        q_in_c.append(_bf16(q_c))
        q_dec_c.append(_bf16(q_c * jnp.exp(base[c])))
        k_in_c.append(_bf16(k_in[rows(c), :]))
        k_st = k[rows(c), :] * jnp.exp(b_last[c] - b[rows(c), :])
        k_st_c.append(k_st)
        k_end_c.append(_bf16(k_st * jnp.exp(base[n_chunks] - base[c + 1])))
    q_dec = jnp.concatenate(q_dec_c, axis=0)
    k_end = jnp.concatenate(k_end_c, axis=0)
    block_decay = jnp.exp(base[n_chunks])

    zero_rows = jnp.zeros((chunk, GLA_K), jnp.bfloat16)
    k_seen = []
    for c in range(n_chunks):
        blocks = [_bf16(k_st_c[j] * jnp.exp(base[c] - base[j + 1])) for j in range(c)]
        blocks.append(k_in_c[c])
        blocks.extend([zero_rows] * (n_chunks - 1 - c))
        k_seen.append(jnp.concatenate(blocks, axis=0))
    return q_in_c, k_seen, q_dec, k_end, block_decay


def _gla_attend(operands, v, r, causal, gla_norm_g, state_ref):
    q_in_c, k_seen, q_dec, k_end, block_decay = operands
    n_chunks = len(q_in_c)
    nt_dims = (((1,), (1,)), ((), ()))
    o_heads = []
    for hd in range(GLA_HEADS):
        ks = slice(hd * GLA_DK, (hd + 1) * GLA_DK)
        vs = slice(hd * GLA_DV, (hd + 1) * GLA_DV)
        att = jnp.concatenate(
            [lax.dot_general(q_in_c[c][:, ks], k_seen[c][:, ks], nt_dims,
                             preferred_element_type=jnp.float32) for c in range(n_chunks)],
            axis=0)
        att = _bf16(jnp.where(causal, att, 0.0))
        st = state_ref[hd]
        oh = _dot(att, v[:, vs]) + _dot(q_dec[:, ks], _bf16(st))
        kv = lax.dot_general(k_end[:, ks], v[:, vs], (((0,), (0,)), ((), ())),
                             preferred_element_type=jnp.float32)
        decay_col = jnp.broadcast_to(block_decay[:, ks], (8, GLA_DK)).T[:, 0:1]
        state_ref[hd] = st * decay_col + kv
        oh = oh * lax.rsqrt(jnp.mean(oh * oh, axis=-1, keepdims=True) + EPS) * gla_norm_g[:, vs]
        o_heads.append(_bf16(oh * _silu(r[:, vs])))
    return jnp.concatenate(o_heads, axis=-1)


def _layer_kernel(x_ref, s0_ref, tail0_ref, norm_g_ref, w_gla_ref, w_conv_ref,
                  w_gate_up_ref, b_gate_ref, gla_norm_g_ref, w_o_gla_ref, conv_w_ref,
                  w_o_conv_ref, w_out_ref, final_g_ref,
                  y_ref, s_out_ref, tail_out_ref,
                  state_ref, tail_ref,
                  *, tile, gla_block, chunk, apply_final_norm):
    t = pl.program_id(1)

    @pl.when(t == 0)
    def _():
        state_ref[...] = s0_ref[...]
        tail_ref[...] = tail0_ref[...]

    x = x_ref[...]
    h = x * lax.rsqrt(jnp.mean(x * x, axis=-1, keepdims=True) + EPS) * norm_g_ref[...]
    h = _bf16(h)

    def proj(name):
        if name in _GLA_SECTIONS:
            w_ref, i = w_gla_ref, _GLA_SECTIONS.index(name)
        else:
            w_ref, i = w_conv_ref, _CONV_SECTIONS.index(name)
        return _dot(h, w_ref[:, i * D_MODEL:(i + 1) * D_MODEL])

    r_glr = _dot(h, w_gla_ref[:, 2 * D_MODEL:])
    r = r_glr[:, :D_MODEL]
    glr = r_glr[:, D_MODEL:D_MODEL + GATE_RANK]
    qk = proj("qk")
    z = _dot(_bf16(glr), w_gate_up_ref[...]) + b_gate_ref[...]
    g = _log_sigmoid(z) * (1.0 / GATE_TAU)
    blocks = [slice(s * gla_block, (s + 1) * gla_block) for s in range(tile // gla_block)]
    causal, tril = _gla_masks(gla_block, chunk)
    gla_norm_g = gla_norm_g_ref[...]

    def prep(blk):
        return _gla_prep(qk[blk, :GLA_K], qk[blk, GLA_K:], g[blk, :], tril, chunk)

    operands = prep(blocks[0])
    v = _bf16(proj("v"))
    gated = []
    for s, blk in enumerate(blocks):
        gated.append(_gla_attend(operands, v[blk, :], r[blk, :], causal, gla_norm_g, state_ref))
        if s + 1 < len(blocks):
            operands = prep(blocks[s + 1])
        if s == 0:
            cc = proj("cc")
            ch = proj("ch")
            cb = proj("cb")
    cz = proj("cz")
    y_gla = _dot(jnp.concatenate(gated, axis=0), w_o_gla_ref[...])

    u = cc * ch
    tail = tail_ref[...]
    tok = lax.broadcasted_iota(jnp.int32, (tile, D_MODEL), 0)
    u1 = jnp.where(tok == 0, tail[1:2, :], pltpu.roll(u, 1, 0))
    u2 = jnp.where(tok == 0, tail[0:1, :],
                   jnp.where(tok == 1, tail[1:2, :], pltpu.roll(u, 2, 0)))
    tail_ref[...] = u[tile - 2:tile, :]
    cw = conv_w_ref[...]
    y_c = cw[0:1, :] * u2 + cw[1:2, :] * u1 + cw[2:3, :] * u
    y_c = cb * y_c * _silu(cz)
    y_conv = _dot(_bf16(y_c), w_o_conv_ref[...])

    merged = _sigmoid(proj("ga")) * y_gla + _sigmoid(proj("gb")) * y_conv
    y = x + _dot(_bf16(merged), w_out_ref[...])
    if apply_final_norm:
        y = y * lax.rsqrt(jnp.mean(y * y, axis=-1, keepdims=True) + EPS) * final_g_ref[...]
    y_ref[...] = y

    @pl.when(t == pl.num_programs(1) - 1)
    def _():
        s_out_ref[...] = state_ref[...]
        tail_out_ref[...] = tail_ref[...]


def _resident(shape):
    return pl.BlockSpec(shape, lambda b, t: (0,) * len(shape), pipeline_mode=pl.Buffered(1))


def _layer_slice(layer, shape):
    return pl.BlockSpec((None,) + shape, lambda b, t: (layer,) + (0,) * len(shape),
                        pipeline_mode=pl.Buffered(1))


def _layer_call(x, s0, tail0, params, final_g, layer, *, tile, chunk, apply_final_norm):
    bsz, seq, _ = x.shape
    gla_block = min(tile, GLA_BLOCK)
    assert seq % tile == 0 and tile % gla_block == 0 and gla_block % chunk == 0
    kern = functools.partial(_layer_kernel, tile=tile, gla_block=gla_block, chunk=chunk,
                             apply_final_norm=apply_final_norm)
    tok_spec = pl.BlockSpec((None, tile, D_MODEL), lambda b, t: (b, t, 0))
    state_shape = (GLA_HEADS, GLA_DK, GLA_DV)
    return pl.pallas_call(
        kern,
        grid=(bsz, seq // tile),
        in_specs=[
            tok_spec,
            _resident(state_shape),
            _resident((CONV_K - 1, D_MODEL)),
            _layer_slice(layer, (1, D_MODEL)),
            _layer_slice(layer, (D_MODEL, N_GLA_COLS + LANES)),
            _layer_slice(layer, (D_MODEL, N_CONV_COLS)),
            _layer_slice(layer, (GATE_RANK, GLA_K)),
            _layer_slice(layer, (1, GLA_K)),
            _layer_slice(layer, (1, GLA_V)),
            _layer_slice(layer, (GLA_V, D_MODEL)),
            _layer_slice(layer, (CONV_K, D_MODEL)),
            _layer_slice(layer, (D_MODEL, D_MODEL)),
            _layer_slice(layer, (D_MODEL, D_MODEL)),
            _resident((1, D_MODEL)),
        ],
        out_specs=[
            tok_spec,
            pl.BlockSpec(state_shape, lambda b, t: (0, 0, 0)),
            pl.BlockSpec((CONV_K - 1, D_MODEL), lambda b, t: (0, 0)),
        ],
        out_shape=[
            jax.ShapeDtypeStruct(x.shape, jnp.float32),
            jax.ShapeDtypeStruct(state_shape, jnp.float32),
            jax.ShapeDtypeStruct((CONV_K - 1, D_MODEL), jnp.float32),
        ],
        scratch_shapes=[
            pltpu.VMEM(state_shape, jnp.float32),
            pltpu.VMEM((CONV_K - 1, D_MODEL), jnp.float32),
        ],
        compiler_params=pltpu.CompilerParams(
            dimension_semantics=("arbitrary", "arbitrary"),
            vmem_limit_bytes=VMEM_LIMIT_BYTES),
        name="hybrid_layer",
    )(x, s0, tail0, *params, final_g)


def _pack_params(norm_g, w_in, w_gate_up, b_gate, gla_norm_g, w_o_gla, conv_w, w_o_conv, w_out):
    depth = w_in.shape[0]
    w_in = _bf16(w_in)
    return (
        norm_g.reshape(depth, 1, D_MODEL),
        w_in,
        w_in[:, :, N_GLA_COLS + GATE_RANK:],
        _bf16(w_gate_up),
        b_gate.reshape(depth, 1, GLA_K),
        gla_norm_g.reshape(depth, 1, GLA_V),
        _bf16(w_o_gla),
        conv_w,
        _bf16(w_o_conv),
        _bf16(w_out),
    )


def kernel(x, meta, norm_g, w_in, w_gate_up, b_gate, gla_norm_g, w_o_gla, conv_w, w_o_conv, w_out,
           final_norm_g):
    depth = w_in.shape[0]
    params = _pack_params(norm_g, w_in, w_gate_up, b_gate, gla_norm_g, w_o_gla, conv_w, w_o_conv,
                          w_out)
    final_g = final_norm_g.reshape(1, D_MODEL)
    hm = meta.astype(x.dtype)[None]
    h = x
    zero_state = jnp.zeros((GLA_HEADS, GLA_DK, GLA_DV), jnp.float32)
    zero_tail = jnp.zeros((CONV_K - 1, D_MODEL), jnp.float32)
    for l in range(depth):
        hm, s_meta, tail_meta = _layer_call(hm, zero_state, zero_tail, params, final_g, l,
                                            tile=N_META, chunk=N_META, apply_final_norm=False)
        h, _, _ = _layer_call(h, s_meta, tail_meta, params, final_g, l,
                              tile=TOKEN_TILE, chunk=CHUNK, apply_final_norm=(l == depth - 1))
    return h
```

```python
import functools

import jax
import jax.numpy as jnp
from jax import lax
from jax.experimental import pallas as pl
from jax.experimental.pallas import tpu as pltpu

D_MODEL = 1024
N_META = 16
GLA_HEADS = 4
GLA_K = D_MODEL // 2
GLA_V = D_MODEL
GLA_DK = GLA_K // GLA_HEADS
GLA_DV = GLA_V // GLA_HEADS
GATE_RANK = 16
GATE_TAU = 16.0
CHUNK = 64
CONV_K = 3
EPS = 1e-6
LANES = 128
BF16_ROWS = 16

_GLA_SECTIONS = ("qk", "v", "r")
_CONV_SECTIONS = ("ch", "cb", "cc", "cz", "ga", "gb")
N_GLA_COLS = len(_GLA_SECTIONS) * D_MODEL
N_CONV_COLS = len(_CONV_SECTIONS) * D_MODEL

VMEM_LIMIT_BYTES = 60 * 1024 * 1024
TOKEN_TILE = 512
GLA_BLOCK = 256


def _sigmoid(z):
    return 1.0 / (1.0 + jnp.exp(-z))


def _silu(z):
    return z * _sigmoid(z)


def _log_sigmoid(z):
    return jnp.minimum(z, 0.0) - jnp.log(1.0 + jnp.exp(-jnp.abs(z)))


def _bf16(a):
    return a.astype(jnp.bfloat16)


def _dot(a, b):
    return jnp.dot(a, b, preferred_element_type=jnp.float32)


def _split2_bf16(a):
    hi = _bf16(a)
    lo = _bf16(a - hi.astype(jnp.float32))
    return hi, lo


def _gla_masks(block, chunk):
    ti = lax.broadcasted_iota(jnp.int32, (block, block), 0)
    si = lax.broadcasted_iota(jnp.int32, (block, block), 1)
    causal = ti >= si
    shift = chunk.bit_length() - 1
    same_chunk = lax.shift_right_logical(ti, shift) == lax.shift_right_logical(si, shift)
    return causal, _bf16((causal & same_chunk).astype(jnp.float32))


def _gla_prep(q, k, g, tril, chunk):
    block = q.shape[0]
    n_chunks = block // chunk
    g_hi, g_lo = _split2_bf16(g)
    b = _dot(tril, g_hi) + _dot(tril, g_lo)

    def rows(c):
        return slice(c * chunk, (c + 1) * chunk)

    b_last = [b[(c + 1) * chunk - 1:(c + 1) * chunk, :] for c in range(n_chunks)]
    base = [jnp.zeros((1, GLA_K), jnp.float32)]
    for c in range(n_chunks):
        base.append(base[c] + b_last[c])

    q_in = q * (jnp.exp(b) * (GLA_DK ** -0.5))
    k_in = k * jnp.exp(-b)
    q_in_c, q_dec_c, k_in_c, k_st_c, k_end_c = [], [], [], [], []
    for c in range(n_chunks):
        q_c = q_in[rows(c), :]
        q_in_c.append(_bf16(q_c))
        q_dec_c.append(_bf16(q_c * jnp.exp(base[c])))
        k_in_c.append(_bf16(k_in[rows(c), :]))
        k_st = k[rows(c), :] * jnp.exp(b_last[c] - b[rows(c), :])
        k_st_c.append(k_st)
        k_end_c.append(_bf16(k_st * jnp.exp(base[n_chunks] - base[c + 1])))
    q_dec = jnp.concatenate(q_dec_c, axis=0)
    k_end = jnp.concatenate(k_end_c, axis=0)
    block_decay = jnp.exp(base[n_chunks])

    zero_rows = jnp.zeros((chunk, GLA_K), jnp.bfloat16)
    k_seen = []
    for c in range(n_chunks):
        blocks = [_bf16(k_st_c[j] * jnp.exp(base[c] - base[j + 1])) for j in range(c)]
        blocks.append(k_in_c[c])
        blocks.extend([zero_rows] * (n_chunks - 1 - c))
        k_seen.append(jnp.concatenate(blocks, axis=0))
    return q_in_c, k_seen, q_dec, k_end, block_decay


def _gla_attend(operands, v, r, causal, gla_norm_g, state_ref):
    q_in_c, k_seen, q_dec, k_end, block_decay = operands
    n_chunks = len(q_in_c)
    nt_dims = (((1,), (1,)), ((), ()))
    o_heads = []
    for hd in range(GLA_HEADS):
        ks = slice(hd * GLA_DK, (hd + 1) * GLA_DK)
        vs = slice(hd * GLA_DV, (hd + 1) * GLA_DV)
        att = jnp.concatenate(
            [lax.dot_general(q_in_c[c][:, ks], k_seen[c][:, ks], nt_dims,
                             preferred_element_type=jnp.float32) for c in range(n_chunks)],
            axis=0)
        att = _bf16(jnp.where(causal, att, 0.0))
        st = state_ref[hd]
        oh = _dot(att, v[:, vs]) + _dot(q_dec[:, ks], _bf16(st))
        kv = lax.dot_general(k_end[:, ks], v[:, vs], (((0,), (0,)), ((), ())),
                             preferred_element_type=jnp.float32)
        decay_col = jnp.broadcast_to(block_decay[:, ks], (8, GLA_DK)).T[:, 0:1]
        state_ref[hd] = st * decay_col + kv
        oh = oh * lax.rsqrt(jnp.mean(oh * oh, axis=-1, keepdims=True) + EPS) * gla_norm_g[:, vs]
        o_heads.append(_bf16(oh * _silu(r[:, vs])))
    return jnp.concatenate(o_heads, axis=-1)


def _layer_kernel(*refs, tile, gla_block, chunk, apply_final_norm, convert_next):
    n_in = 14 + (len(_NEXT_F32) if convert_next else 0)
    (x_ref, s0_ref, tail0_ref, norm_g_ref, w_gate_up_ref, b_gate_ref, gla_norm_g_ref, conv_w_ref,
     final_g_ref, w_gla_ref, w_conv_ref, w_o_gla_ref, w_o_conv_ref, w_out_ref) = refs[:14]
    y_ref, s_out_ref, tail_out_ref = refs[n_in:n_in + 3]
    state_ref, tail_ref = refs[-2:]
    t = pl.program_id(1)

    @pl.when(t == 0)
    def _():
        state_ref[...] = s0_ref[...]
        tail_ref[...] = tail0_ref[...]

    x = x_ref[...]
    h = x * lax.rsqrt(jnp.mean(x * x, axis=-1, keepdims=True) + EPS) * norm_g_ref[...]
    h = _bf16(h)

    def proj(name):
        if name in _GLA_SECTIONS:
            w_ref, i = w_gla_ref, _GLA_SECTIONS.index(name)
        else:
            w_ref, i = w_conv_ref, _CONV_SECTIONS.index(name)
        return _dot(h, w_ref[:, i * D_MODEL:(i + 1) * D_MODEL])

    r_glr = _dot(h, w_gla_ref[:, 2 * D_MODEL:])
    r = r_glr[:, :D_MODEL]
    glr = r_glr[:, D_MODEL:D_MODEL + GATE_RANK]
    qk = proj("qk")
    z = _dot(_bf16(glr), w_gate_up_ref[...]) + b_gate_ref[...]
    g = _log_sigmoid(z) * (1.0 / GATE_TAU)
    blocks = [slice(s * gla_block, (s + 1) * gla_block) for s in range(tile // gla_block)]
    causal, tril = _gla_masks(gla_block, chunk)
    gla_norm_g = gla_norm_g_ref[...]

    def prep(blk):
        return _gla_prep(qk[blk, :GLA_K], qk[blk, GLA_K:], g[blk, :], tril, chunk)

    operands = prep(blocks[0])
    v = _bf16(proj("v"))
    gated = []
    for s, blk in enumerate(blocks):
        gated.append(_gla_attend(operands, v[blk, :], r[blk, :], causal, gla_norm_g, state_ref))
        if s + 1 < len(blocks):
            operands = prep(blocks[s + 1])
        if s == 0:
            cc = proj("cc")
            ch = proj("ch")
            cb = proj("cb")
    cz = proj("cz")
    if convert_next:
        w_in_ref, *square_refs = refs[14:n_in]
        o_gla_ref, o_conv_ref, *o_square_refs = refs[n_in + 3:-2]
        w_rows = w_in_ref[...]
        o_gla_ref[...] = _bf16(w_rows[:, :N_GLA_COLS + LANES])
        o_conv_ref[...] = _bf16(w_rows[:, N_GLA_COLS + GATE_RANK:])
        for src_ref, dst_ref in zip(square_refs, o_square_refs):
            dst_ref[...] = _bf16(src_ref[...])
    y_gla = _dot(jnp.concatenate(gated, axis=0), w_o_gla_ref[...])

    u = cc * ch
    tail = tail_ref[...]
    tok = lax.broadcasted_iota(jnp.int32, (tile, D_MODEL), 0)
    u1 = jnp.where(tok == 0, tail[1:2, :], pltpu.roll(u, 1, 0))
    u2 = jnp.where(tok == 0, tail[0:1, :],
                   jnp.where(tok == 1, tail[1:2, :], pltpu.roll(u, 2, 0)))
    tail_ref[...] = u[tile - 2:tile, :]
    cw = conv_w_ref[...]
    y_c = cw[0:1, :] * u2 + cw[1:2, :] * u1 + cw[2:3, :] * u
    y_c = cb * y_c * _silu(cz)
    y_conv = _dot(_bf16(y_c), w_o_conv_ref[...])

    merged = _sigmoid(proj("ga")) * y_gla + _sigmoid(proj("gb")) * y_conv
    y = x + _dot(_bf16(merged), w_out_ref[...])
    if apply_final_norm:
        y = y * lax.rsqrt(jnp.mean(y * y, axis=-1, keepdims=True) + EPS) * final_g_ref[...]
    y_ref[...] = y

    @pl.when(t == pl.num_programs(1) - 1)
    def _():
        s_out_ref[...] = state_ref[...]
        tail_out_ref[...] = tail_ref[...]


def _resident(shape):
    return pl.BlockSpec(shape, lambda b, t: (0,) * len(shape), pipeline_mode=pl.Buffered(1))


def _layer_slice(layer, shape):
    return pl.BlockSpec((None,) + shape, lambda b, t: (layer,) + (0,) * len(shape),
                        pipeline_mode=pl.Buffered(1))


_NEXT_F32 = ("w_in", "w_o_gla", "w_o_conv", "w_out")


def _layer_call(x, s0, tail0, small, big, final_g, layer, next_f32, *, tile, chunk,
                apply_final_norm):
    bsz, seq, _ = x.shape
    n_tiles = seq // tile
    gla_block = min(tile, GLA_BLOCK)
    assert seq % tile == 0 and tile % gla_block == 0 and gla_block % chunk == 0
    convert_next = next_f32 is not None
    kern = functools.partial(_layer_kernel, tile=tile, gla_block=gla_block, chunk=chunk,
                             apply_final_norm=apply_final_norm, convert_next=convert_next)
    tok_spec = pl.BlockSpec((None, tile, D_MODEL), lambda b, t: (b, t, 0))
    state_shape = (GLA_HEADS, GLA_DK, GLA_DV)
    in_specs = [
        tok_spec,
        _resident(state_shape),
        _resident((CONV_K - 1, D_MODEL)),
        _layer_slice(layer, (1, D_MODEL)),
        _layer_slice(layer, (GATE_RANK, GLA_K)),
        _layer_slice(layer, (1, GLA_K)),
        _layer_slice(layer, (1, GLA_V)),
        _layer_slice(layer, (CONV_K, D_MODEL)),
        _resident((1, D_MODEL)),
        _resident((D_MODEL, N_GLA_COLS + LANES)),
        _resident((D_MODEL, N_CONV_COLS)),
        _resident((GLA_V, D_MODEL)),
        _resident((D_MODEL, D_MODEL)),
        _resident((D_MODEL, D_MODEL)),
    ]
    out_specs = [
        tok_spec,
        pl.BlockSpec(state_shape, lambda b, t: (0, 0, 0)),
        pl.BlockSpec((CONV_K - 1, D_MODEL), lambda b, t: (0, 0)),
    ]
    out_shape = [
        jax.ShapeDtypeStruct(x.shape, jnp.float32),
        jax.ShapeDtypeStruct(state_shape, jnp.float32),
        jax.ShapeDtypeStruct((CONV_K - 1, D_MODEL), jnp.float32),
    ]
    args = [x, s0, tail0, *small, final_g, *big]
    if convert_next:
        steps = bsz * n_tiles
        assert D_MODEL % steps == 0 and (D_MODEL // steps) % BF16_ROWS == 0
        rows = D_MODEL // steps
        for w in next_f32:
            in_specs.append(pl.BlockSpec((None, rows, w.shape[-1]),
                                         lambda b, t: (layer + 1, b * n_tiles + t, 0)))
        for cols in (N_GLA_COLS + LANES, N_CONV_COLS, D_MODEL, D_MODEL, D_MODEL):
            out_specs.append(pl.BlockSpec((rows, cols), lambda b, t: (b * n_tiles + t, 0)))
            out_shape.append(jax.ShapeDtypeStruct((D_MODEL, cols), jnp.bfloat16))
        args.extend(next_f32)
    return pl.pallas_call(
        kern,
        grid=(bsz, n_tiles),
        in_specs=in_specs,
        out_specs=out_specs,
        out_shape=out_shape,
        scratch_shapes=[
            pltpu.VMEM(state_shape, jnp.float32),
            pltpu.VMEM((CONV_K - 1, D_MODEL), jnp.float32),
        ],
        compiler_params=pltpu.CompilerParams(
            dimension_semantics=("arbitrary", "arbitrary"),
            vmem_limit_bytes=VMEM_LIMIT_BYTES),
        name="hybrid_layer",
    )(*args)


def kernel(x, meta, norm_g, w_in, w_gate_up, b_gate, gla_norm_g, w_o_gla, conv_w, w_o_conv, w_out,
           final_norm_g):
    depth = w_in.shape[0]
    small = (norm_g.reshape(depth, 1, D_MODEL), _bf16(w_gate_up), b_gate.reshape(depth, 1, GLA_K),
             gla_norm_g.reshape(depth, 1, GLA_V), conv_w)
    next_f32 = (w_in, w_o_gla, w_o_conv, w_out)
    w_in0 = _bf16(w_in[0])
    big = (w_in0, w_in0[:, N_GLA_COLS + GATE_RANK:], _bf16(w_o_gla[0]), _bf16(w_o_conv[0]),
           _bf16(w_out[0]))
    final_g = final_norm_g.reshape(1, D_MODEL)
    hm = meta.astype(x.dtype)[None]
    h = x
    zero_state = jnp.zeros((GLA_HEADS, GLA_DK, GLA_DV), jnp.float32)
    zero_tail = jnp.zeros((CONV_K - 1, D_MODEL), jnp.float32)
    for l in range(depth):
        last = l == depth - 1
        hm, s_meta, tail_meta = _layer_call(hm, zero_state, zero_tail, small, big, final_g, l, None,
                                            tile=N_META, chunk=N_META, apply_final_norm=False)
        h, _, _, *big = _layer_call(h, s_meta, tail_meta, small, big, final_g, l,
                                    None if last else next_f32,
                                    tile=TOKEN_TILE, chunk=CHUNK, apply_final_norm=last)
    return h
```

```python
import functools

import jax
import jax.numpy as jnp
from jax import lax
from jax.experimental import pallas as pl
from jax.experimental.pallas import tpu as pltpu

D_MODEL = 1024
N_META = 16
GLA_HEADS = 4
GLA_K = D_MODEL // 2
GLA_V = D_MODEL
GLA_DK = GLA_K // GLA_HEADS
GLA_DV = GLA_V // GLA_HEADS
GATE_RANK = 16
GATE_TAU = 16.0
CHUNK = 64
CONV_K = 3
EPS = 1e-6
LANES = 128
BF16_ROWS = 16

_GLA_SECTIONS = ("qk", "v", "r")
_CONV_SECTIONS = ("ch", "cb", "cc", "cz", "ga", "gb")
N_GLA_COLS = len(_GLA_SECTIONS) * D_MODEL
N_CONV_COLS = len(_CONV_SECTIONS) * D_MODEL

VMEM_LIMIT_BYTES = 60 * 1024 * 1024
TOKEN_TILE = 512
GLA_BLOCK = 256


def _sigmoid(z):
    return 1.0 / (1.0 + jnp.exp(-z))


def _silu(z):
    return z * _sigmoid(z)


def _log_sigmoid(z):
    return jnp.minimum(z, 0.0) - jnp.log(1.0 + jnp.exp(-jnp.abs(z)))


def _bf16(a):
    return a.astype(jnp.bfloat16)


def _dot(a, b):
    return jnp.dot(a, b, preferred_element_type=jnp.float32)


def _split2_bf16(a):
    hi = _bf16(a)
    lo = _bf16(a - hi.astype(jnp.float32))
    return hi, lo


def _gla_masks(block, chunk):
    ti = lax.broadcasted_iota(jnp.int32, (block, block), 0)
    si = lax.broadcasted_iota(jnp.int32, (block, block), 1)
    causal = ti >= si
    shift = chunk.bit_length() - 1
    same_chunk = lax.shift_right_logical(ti, shift) == lax.shift_right_logical(si, shift)
    return causal, _bf16((causal & same_chunk).astype(jnp.float32))


def _gla_prep(q, k, g, tril, chunk):
    block = q.shape[0]
    n_chunks = block // chunk
    g_hi, g_lo = _split2_bf16(g)
    b = _dot(tril, g_hi) + _dot(tril, g_lo)

    def rows(c):
        return slice(c * chunk, (c + 1) * chunk)

    b_last = [b[(c + 1) * chunk - 1:(c + 1) * chunk, :] for c in range(n_chunks)]
    base = [jnp.zeros((1, GLA_K), jnp.float32)]
    for c in range(n_chunks):
        base.append(base[c] + b_last[c])

    q_in = q * (jnp.exp(b) * (GLA_DK ** -0.5))
    k_in = k * jnp.exp(-b)
    q_in_c, q_dec_c, k_in_c, k_st_c, k_end_c = [], [], [], [], []
    for c in range(n_chunks):
        q_c = q_in[rows(c), :]
        q_in_c.append(_bf16(q_c))
        q_dec_c.append(_bf16(q_c * jnp.exp(base[c])))
        k_in_c.append(_bf16(k_in[rows(c), :]))
        k_st = k[rows(c), :] * jnp.exp(b_last[c] - b[rows(c), :])
        k_st_c.append(k_st)
        k_end_c.append(_bf16(k_st * jnp.exp(base[n_chunks] - base[c + 1])))
    q_dec = jnp.concatenate(q_dec_c, axis=0)
    k_end = jnp.concatenate(k_end_c, axis=0)
    block_decay = jnp.exp(base[n_chunks])

    zero_rows = jnp.zeros((chunk, GLA_K), jnp.bfloat16)
    k_seen = []
    for c in range(n_chunks):
        blocks = [_bf16(k_st_c[j] * jnp.exp(base[c] - base[j + 1])) for j in range(c)]
        blocks.append(k_in_c[c])
        blocks.extend([zero_rows] * (n_chunks - 1 - c))
        k_seen.append(jnp.concatenate(blocks, axis=0))
    return q_in_c, k_seen, q_dec, k_end, block_decay


def _gla_attend(operands, v, r, causal, gla_norm_g, state_ref):
    q_in_c, k_seen, q_dec, k_end, block_decay = operands
    n_chunks = len(q_in_c)
    nt_dims = (((1,), (1,)), ((), ()))
    o_heads = []
    for hd in range(GLA_HEADS):
        ks = slice(hd * GLA_DK, (hd + 1) * GLA_DK)
        vs = slice(hd * GLA_DV, (hd + 1) * GLA_DV)
        att = jnp.concatenate(
            [lax.dot_general(q_in_c[c][:, ks], k_seen[c][:, ks], nt_dims,
                             preferred_element_type=jnp.float32) for c in range(n_chunks)],
            axis=0)
        att = _bf16(jnp.where(causal, att, 0.0))
        st = state_ref[hd]
        oh = _dot(att, v[:, vs]) + _dot(q_dec[:, ks], _bf16(st))
        kv = lax.dot_general(k_end[:, ks], v[:, vs], (((0,), (0,)), ((), ())),
                             preferred_element_type=jnp.float32)
        decay_col = jnp.broadcast_to(block_decay[:, ks], (8, GLA_DK)).T[:, 0:1]
        state_ref[hd] = st * decay_col + kv
        oh = oh * lax.rsqrt(jnp.mean(oh * oh, axis=-1, keepdims=True) + EPS) * gla_norm_g[:, vs]
        o_heads.append(_bf16(oh * _silu(r[:, vs])))
    return jnp.concatenate(o_heads, axis=-1)


def _cast_weight_rows(f32_refs, bf16_refs):
    w_in_ref, *square_refs = f32_refs
    o_gla_ref, o_conv_ref, *o_square_refs = bf16_refs
    w_rows = w_in_ref[...]
    o_gla_ref[...] = _bf16(w_rows[:, :N_GLA_COLS + LANES])
    o_conv_ref[...] = _bf16(w_rows[:, N_GLA_COLS + GATE_RANK:])
    for src_ref, dst_ref in zip(square_refs, o_square_refs):
        dst_ref[...] = _bf16(src_ref[...])


def _cast_kernel(*refs):
    _cast_weight_rows(refs[:len(_NEXT_F32)], refs[len(_NEXT_F32):])


def _layer_kernel(*refs, tile, gla_block, chunk, apply_final_norm, convert_next):
    n_in = 14 + (len(_NEXT_F32) if convert_next else 0)
    (x_ref, s0_ref, tail0_ref, norm_g_ref, w_gate_up_ref, b_gate_ref, gla_norm_g_ref, conv_w_ref,
     final_g_ref, w_gla_ref, w_conv_ref, w_o_gla_ref, w_o_conv_ref, w_out_ref) = refs[:14]
    y_ref, s_out_ref, tail_out_ref = refs[n_in:n_in + 3]
    state_ref, tail_ref = refs[-2:]
    t = pl.program_id(1)

    @pl.when(t == 0)
    def _():
        state_ref[...] = s0_ref[...]
        tail_ref[...] = tail0_ref[...]

    x = x_ref[...]
    h = x * lax.rsqrt(jnp.mean(x * x, axis=-1, keepdims=True) + EPS) * norm_g_ref[...]
    h = _bf16(h)

    def proj(name):
        if name in _GLA_SECTIONS:
            w_ref, i = w_gla_ref, _GLA_SECTIONS.index(name)
        else:
            w_ref, i = w_conv_ref, _CONV_SECTIONS.index(name)
        return _dot(h, w_ref[:, i * D_MODEL:(i + 1) * D_MODEL])

    r_glr = _dot(h, w_gla_ref[:, 2 * D_MODEL:])
    r = r_glr[:, :D_MODEL]
    glr = r_glr[:, D_MODEL:D_MODEL + GATE_RANK]
    qk = proj("qk")
    z = _dot(_bf16(glr), w_gate_up_ref[...]) + b_gate_ref[...]
    g = _log_sigmoid(z) * (1.0 / GATE_TAU)
    blocks = [slice(s * gla_block, (s + 1) * gla_block) for s in range(tile // gla_block)]
    causal, tril = _gla_masks(gla_block, chunk)
    gla_norm_g = gla_norm_g_ref[...]

    def prep(blk):
        return _gla_prep(qk[blk, :GLA_K], qk[blk, GLA_K:], g[blk, :], tril, chunk)

    operands = prep(blocks[0])
    v = _bf16(proj("v"))
    gated = []
    for s, blk in enumerate(blocks):
        gated.append(_gla_attend(operands, v[blk, :], r[blk, :], causal, gla_norm_g, state_ref))
        if s + 1 < len(blocks):
            operands = prep(blocks[s + 1])
        if s == 0:
            cc = proj("cc")
            ch = proj("ch")
            cb = proj("cb")
    cz = proj("cz")
    if convert_next:
        _cast_weight_rows(refs[14:n_in], refs[n_in + 3:-2])
    y_gla = _dot(jnp.concatenate(gated, axis=0), w_o_gla_ref[...])

    u = cc * ch
    tail = tail_ref[...]
    tok = lax.broadcasted_iota(jnp.int32, (tile, D_MODEL), 0)
    u1 = jnp.where(tok == 0, tail[1:2, :], pltpu.roll(u, 1, 0))
    u2 = jnp.where(tok == 0, tail[0:1, :],
                   jnp.where(tok == 1, tail[1:2, :], pltpu.roll(u, 2, 0)))
    tail_ref[...] = u[tile - 2:tile, :]
    cw = conv_w_ref[...]
    y_c = cw[0:1, :] * u2 + cw[1:2, :] * u1 + cw[2:3, :] * u
    y_c = cb * y_c * _silu(cz)
    y_conv = _dot(_bf16(y_c), w_o_conv_ref[...])

    merged = _sigmoid(proj("ga")) * y_gla + _sigmoid(proj("gb")) * y_conv
    y = x + _dot(_bf16(merged), w_out_ref[...])
    if apply_final_norm:
        y = y * lax.rsqrt(jnp.mean(y * y, axis=-1, keepdims=True) + EPS) * final_g_ref[...]
    y_ref[...] = y

    @pl.when(t == pl.num_programs(1) - 1)
    def _():
        s_out_ref[...] = state_ref[...]
        tail_out_ref[...] = tail_ref[...]


def _resident(shape):
    return pl.BlockSpec(shape, lambda b, t: (0,) * len(shape), pipeline_mode=pl.Buffered(1))


def _layer_slice(layer, shape):
    return pl.BlockSpec((None,) + shape, lambda b, t: (layer,) + (0,) * len(shape),
                        pipeline_mode=pl.Buffered(1))


_NEXT_F32 = ("w_in", "w_o_gla", "w_o_conv", "w_out")
_CAST_COLS = (N_GLA_COLS + LANES, N_CONV_COLS, D_MODEL, D_MODEL, D_MODEL)
CAST_ROWS = 64


def _cast_specs(f32_mats, layer, rows, row_block):
    assert rows % BF16_ROWS == 0
    in_specs = [pl.BlockSpec((None, rows, w.shape[-1]), lambda *g: (layer, row_block(*g), 0))
                for w in f32_mats]
    out_specs = [pl.BlockSpec((rows, cols), lambda *g: (row_block(*g), 0)) for cols in _CAST_COLS]
    out_shape = [jax.ShapeDtypeStruct((D_MODEL, cols), jnp.bfloat16) for cols in _CAST_COLS]
    return in_specs, out_specs, out_shape


def _cast_layer(f32_mats, layer):
    in_specs, out_specs, out_shape = _cast_specs(f32_mats, layer, CAST_ROWS, lambda i: i)
    return pl.pallas_call(
        _cast_kernel,
        grid=(D_MODEL // CAST_ROWS,),
        in_specs=in_specs,
        out_specs=out_specs,
        out_shape=out_shape,
        compiler_params=pltpu.CompilerParams(dimension_semantics=("arbitrary",)),
        name="cast_weights",
    )(*f32_mats)


def _layer_call(x, s0, tail0, small, big, final_g, layer, next_f32, *, tile, chunk,
                apply_final_norm):
    bsz, seq, _ = x.shape
    n_tiles = seq // tile
    gla_block = min(tile, GLA_BLOCK)
    assert seq % tile == 0 and tile % gla_block == 0 and gla_block % chunk == 0
    convert_next = next_f32 is not None
    kern = functools.partial(_layer_kernel, tile=tile, gla_block=gla_block, chunk=chunk,
                             apply_final_norm=apply_final_norm, convert_next=convert_next)
    tok_spec = pl.BlockSpec((None, tile, D_MODEL), lambda b, t: (b, t, 0))
    state_shape = (GLA_HEADS, GLA_DK, GLA_DV)
    in_specs = [
        tok_spec,
        _resident(state_shape),
        _resident((CONV_K - 1, D_MODEL)),
        _layer_slice(layer, (1, D_MODEL)),
        _layer_slice(layer, (GATE_RANK, GLA_K)),
        _layer_slice(layer, (1, GLA_K)),
        _layer_slice(layer, (1, GLA_V)),
        _layer_slice(layer, (CONV_K, D_MODEL)),
        _resident((1, D_MODEL)),
        _resident((D_MODEL, N_GLA_COLS + LANES)),
        _resident((D_MODEL, N_CONV_COLS)),
        _resident((GLA_V, D_MODEL)),
        _resident((D_MODEL, D_MODEL)),
        _resident((D_MODEL, D_MODEL)),
    ]
    out_specs = [
        tok_spec,
        pl.BlockSpec(state_shape, lambda b, t: (0, 0, 0)),
        pl.BlockSpec((CONV_K - 1, D_MODEL), lambda b, t: (0, 0)),
    ]
    out_shape = [
        jax.ShapeDtypeStruct(x.shape, jnp.float32),
        jax.ShapeDtypeStruct(state_shape, jnp.float32),
        jax.ShapeDtypeStruct((CONV_K - 1, D_MODEL), jnp.float32),
    ]
    args = [x, s0, tail0, *small, final_g, *big]
    if convert_next:
        steps = bsz * n_tiles
        assert D_MODEL % steps == 0
        cast = _cast_specs(next_f32, layer + 1, D_MODEL // steps, lambda b, t: b * n_tiles + t)
        in_specs.extend(cast[0])
        out_specs.extend(cast[1])
        out_shape.extend(cast[2])
        args.extend(next_f32)
    return pl.pallas_call(
        kern,
        grid=(bsz, n_tiles),
        in_specs=in_specs,
        out_specs=out_specs,
        out_shape=out_shape,
        scratch_shapes=[
            pltpu.VMEM(state_shape, jnp.float32),
            pltpu.VMEM((CONV_K - 1, D_MODEL), jnp.float32),
        ],
        compiler_params=pltpu.CompilerParams(
            dimension_semantics=("arbitrary", "arbitrary"),
            vmem_limit_bytes=VMEM_LIMIT_BYTES),
        name="hybrid_layer",
    )(*args)


def kernel(x, meta, norm_g, w_in, w_gate_up, b_gate, gla_norm_g, w_o_gla, conv_w, w_o_conv, w_out,
           final_norm_g):
    depth = w_in.shape[0]
    small = (norm_g.reshape(depth, 1, D_MODEL), _bf16(w_gate_up), b_gate.reshape(depth, 1, GLA_K),
             gla_norm_g.reshape(depth, 1, GLA_V), conv_w)
    next_f32 = (w_in, w_o_gla, w_o_conv, w_out)
    big = _cast_layer(next_f32, 0)
    final_g = final_norm_g.reshape(1, D_MODEL)
    hm = meta.astype(x.dtype)[None]
    h = x
    zero_state = jnp.zeros((GLA_HEADS, GLA_DK, GLA_DV), jnp.float32)
    zero_tail = jnp.zeros((CONV_K - 1, D_MODEL), jnp.float32)
    for l in range(depth):
        last = l == depth - 1
        hm, s_meta, tail_meta = _layer_call(hm, zero_state, zero_tail, small, big, final_g, l, None,
                                            tile=N_META, chunk=N_META, apply_final_norm=False)
        h, _, _, *big = _layer_call(h, s_meta, tail_meta, small, big, final_g, l,
                                    None if last else next_f32,
                                    tile=TOKEN_TILE, chunk=CHUNK, apply_final_norm=last)
    return h
```

```python
import functools

import jax
import jax.numpy as jnp
from jax import lax
from jax.experimental import pallas as pl
from jax.experimental.pallas import tpu as pltpu

D_MODEL = 1024
N_META = 16
GLA_HEADS = 4
GLA_K = D_MODEL // 2
GLA_V = D_MODEL
GLA_DK = GLA_K // GLA_HEADS
GLA_DV = GLA_V // GLA_HEADS
GATE_RANK = 16
GATE_TAU = 16.0
CHUNK = 64
CONV_K = 3
EPS = 1e-6
LANES = 128
BF16_ROWS = 16

_GLA_SECTIONS = ("qk", "v", "r")
_CONV_SECTIONS = ("ch", "cb", "cc", "cz", "ga", "gb")
N_GLA_COLS = len(_GLA_SECTIONS) * D_MODEL
N_CONV_COLS = len(_CONV_SECTIONS) * D_MODEL

VMEM_LIMIT_BYTES = 60 * 1024 * 1024
TOKEN_TILE = 512
GLA_BLOCK = 256


def _sigmoid(z):
    return 1.0 / (1.0 + jnp.exp(-z))


def _silu(z):
    return z * _sigmoid(z)


def _log_sigmoid(z):
    return jnp.minimum(z, 0.0) - jnp.log(1.0 + jnp.exp(-jnp.abs(z)))


def _bf16(a):
    return a.astype(jnp.bfloat16)


def _dot(a, b):
    return jnp.dot(a, b, preferred_element_type=jnp.float32)


def _split2_bf16(a):
    hi = _bf16(a)
    lo = _bf16(a - hi.astype(jnp.float32))
    return hi, lo


def _gla_masks(block, chunk):
    ti = lax.broadcasted_iota(jnp.int32, (block, block), 0)
    si = lax.broadcasted_iota(jnp.int32, (block, block), 1)
    causal = ti >= si
    shift = chunk.bit_length() - 1
    same_chunk = lax.shift_right_logical(ti, shift) == lax.shift_right_logical(si, shift)
    return causal, _bf16((causal & same_chunk).astype(jnp.float32))


def _gla_prep(q, k, g, tril, chunk):
    block = q.shape[0]
    n_chunks = block // chunk
    g_hi, g_lo = _split2_bf16(g)
    b = _dot(tril, g_hi) + _dot(tril, g_lo)

    def rows(c):
        return slice(c * chunk, (c + 1) * chunk)

    b_last = [b[(c + 1) * chunk - 1:(c + 1) * chunk, :] for c in range(n_chunks)]
    base = [jnp.zeros((1, GLA_K), jnp.float32)]
    for c in range(n_chunks):
        base.append(base[c] + b_last[c])

    q_in = q * (jnp.exp(b) * (GLA_DK ** -0.5))
    k_in = k * jnp.exp(-b)
    q_in_c, q_dec_c, k_in_c, k_st_c, k_end_c = [], [], [], [], []
    for c in range(n_chunks):
        q_c = q_in[rows(c), :]
        q_in_c.append(_bf16(q_c))
        q_dec_c.append(_bf16(q_c * jnp.exp(base[c])))
        k_in_c.append(_bf16(k_in[rows(c), :]))
        k_st = k[rows(c), :] * jnp.exp(b_last[c] - b[rows(c), :])
        k_st_c.append(k_st)
        k_end_c.append(_bf16(k_st * jnp.exp(base[n_chunks] - base[c + 1])))
    q_dec = jnp.concatenate(q_dec_c, axis=0)
    k_end = jnp.concatenate(k_end_c, axis=0)
    block_decay = jnp.exp(base[n_chunks])

    zero_rows = jnp.zeros((chunk, GLA_K), jnp.bfloat16)
    k_seen = []
    for c in range(n_chunks):
        blocks = [_bf16(k_st_c[j] * jnp.exp(base[c] - base[j + 1])) for j in range(c)]
        blocks.append(k_in_c[c])
        blocks.extend([zero_rows] * (n_chunks - 1 - c))
        k_seen.append(jnp.concatenate(blocks, axis=0))
    return q_in_c, k_seen, q_dec, k_end, block_decay


def _gla_attend(operands, v, r, causal, gla_norm_g, state_ref):
    q_in_c, k_seen, q_dec, k_end, block_decay = operands
    n_chunks = len(q_in_c)
    nt_dims = (((1,), (1,)), ((), ()))
    o_heads = []
    for hd in range(GLA_HEADS):
        ks = slice(hd * GLA_DK, (hd + 1) * GLA_DK)
        vs = slice(hd * GLA_DV, (hd + 1) * GLA_DV)
        att = jnp.concatenate(
            [lax.dot_general(q_in_c[c][:, ks], k_seen[c][:, ks], nt_dims,
                             preferred_element_type=jnp.float32) for c in range(n_chunks)],
            axis=0)
        att = _bf16(jnp.where(causal, att, 0.0))
        st = state_ref[hd]
        oh = _dot(att, v[:, vs]) + _dot(q_dec[:, ks], _bf16(st))
        kv = lax.dot_general(k_end[:, ks], v[:, vs], (((0,), (0,)), ((), ())),
                             preferred_element_type=jnp.float32)
        decay_col = jnp.broadcast_to(block_decay[:, ks], (8, GLA_DK)).T[:, 0:1]
        state_ref[hd] = st * decay_col + kv
        oh = oh * lax.rsqrt(jnp.mean(oh * oh, axis=-1, keepdims=True) + EPS) * gla_norm_g[:, vs]
        o_heads.append(_bf16(oh * _silu(r[:, vs])))
    return jnp.concatenate(o_heads, axis=-1)


def _cast_w_in_blocks(f32_refs, bf16_refs):
    wt_conv_ref, wt_gla_ref = f32_refs[:2]
    o_gla_ref, o_conv_ref = bf16_refs[:2]
    o_conv_ref[...] = _bf16(wt_conv_ref[0].T)
    o_gla_ref[...] = _bf16(wt_gla_ref[...].T)


def _cast_square_rows(f32_refs, bf16_refs):
    for src_ref, dst_ref in zip(f32_refs[2:], bf16_refs[2:]):
        dst_ref[...] = _bf16(src_ref[...])


def _cast_kernel(*refs):
    _cast_w_in_blocks(refs[:N_CAST_IN], refs[N_CAST_IN:])
    _cast_square_rows(refs[:N_CAST_IN], refs[N_CAST_IN:])


def _layer_kernel(*refs, tile, gla_block, chunk, apply_final_norm, convert_next):
    n_in = 14 + (N_CAST_IN if convert_next else 0)
    (x_ref, s0_ref, tail0_ref, norm_g_ref, w_gate_up_ref, b_gate_ref, gla_norm_g_ref, conv_w_ref,
     final_g_ref, w_gla_ref, w_conv_ref, w_o_gla_ref, w_o_conv_ref, w_out_ref) = refs[:14]
    y_ref, s_out_ref, tail_out_ref = refs[n_in:n_in + 3]
    state_ref, tail_ref = refs[-2:]
    t = pl.program_id(1)

    @pl.when(t == 0)
    def _():
        state_ref[...] = s0_ref[...]
        tail_ref[...] = tail0_ref[...]

    x = x_ref[...]
    h = x * lax.rsqrt(jnp.mean(x * x, axis=-1, keepdims=True) + EPS) * norm_g_ref[...]
    h = _bf16(h)

    def proj(name):
        if name in _GLA_SECTIONS:
            w_ref, i = w_gla_ref, _GLA_SECTIONS.index(name)
        else:
            w_ref, i = w_conv_ref, _CONV_SECTIONS.index(name)
        return _dot(h, w_ref[:, i * D_MODEL:(i + 1) * D_MODEL])

    r_glr = _dot(h, w_gla_ref[:, 2 * D_MODEL:])
    r = r_glr[:, :D_MODEL]
    glr = r_glr[:, D_MODEL:D_MODEL + GATE_RANK]
    qk = proj("qk")
    z = _dot(_bf16(glr), w_gate_up_ref[...]) + b_gate_ref[...]
    g = _log_sigmoid(z) * (1.0 / GATE_TAU)
    blocks = [slice(s * gla_block, (s + 1) * gla_block) for s in range(tile // gla_block)]
    causal, tril = _gla_masks(gla_block, chunk)
    gla_norm_g = gla_norm_g_ref[...]

    def prep(blk):
        return _gla_prep(qk[blk, :GLA_K], qk[blk, GLA_K:], g[blk, :], tril, chunk)

    operands = prep(blocks[0])
    v = _bf16(proj("v"))
    gated = []
    for s, blk in enumerate(blocks):
        gated.append(_gla_attend(operands, v[blk, :], r[blk, :], causal, gla_norm_g, state_ref))
        if s + 1 < len(blocks):
            operands = prep(blocks[s + 1])
        if s == 0:
            cc = proj("cc")
            ch = proj("ch")
            cb = proj("cb")
    cz = proj("cz")
    if convert_next:
        _cast_square_rows(refs[14:n_in], refs[n_in + 3:-2])
    y_gla = _dot(jnp.concatenate(gated, axis=0), w_o_gla_ref[...])

    u = cc * ch
    tail = tail_ref[...]
    tok = lax.broadcasted_iota(jnp.int32, (tile, D_MODEL), 0)
    u1 = jnp.where(tok == 0, tail[1:2, :], pltpu.roll(u, 1, 0))
    u2 = jnp.where(tok == 0, tail[0:1, :],
                   jnp.where(tok == 1, tail[1:2, :], pltpu.roll(u, 2, 0)))
    tail_ref[...] = u[tile - 2:tile, :]
    cw = conv_w_ref[...]
    y_c = cw[0:1, :] * u2 + cw[1:2, :] * u1 + cw[2:3, :] * u
    y_c = cb * y_c * _silu(cz)
    y_conv = _dot(_bf16(y_c), w_o_conv_ref[...])

    merged = _sigmoid(proj("ga")) * y_gla + _sigmoid(proj("gb")) * y_conv
    y = x + _dot(_bf16(merged), w_out_ref[...])
    if apply_final_norm:
        y = y * lax.rsqrt(jnp.mean(y * y, axis=-1, keepdims=True) + EPS) * final_g_ref[...]
    y_ref[...] = y

    @pl.when(t == pl.num_programs(1) - 1)
    def _():
        s_out_ref[...] = state_ref[...]
        tail_out_ref[...] = tail_ref[...]

    if convert_next:
        @pl.when(pl.program_id(0) * pl.num_programs(1) + t < N_CAST_STEPS)
        def _():
            _cast_w_in_blocks(refs[14:n_in], refs[n_in + 3:-2])


def _resident(shape):
    return pl.BlockSpec(shape, lambda b, t: (0,) * len(shape), pipeline_mode=pl.Buffered(1))


def _layer_slice(layer, shape):
    return pl.BlockSpec((None,) + shape, lambda b, t: (layer,) + (0,) * len(shape),
                        pipeline_mode=pl.Buffered(1))


N_CAST_IN = 5
_CAST_COLS = (N_GLA_COLS + LANES, N_CONV_COLS, D_MODEL, D_MODEL, D_MODEL)
CONV_CAST_COLS = 256
N_CAST_STEPS = max(N_CONV_COLS // CONV_CAST_COLS, (N_GLA_COLS + LANES) // LANES)
F32_ROWS = 8


def _cast_specs(layer, step_of, square_rows, square_block):
    assert square_rows % BF16_ROWS == 0 and (N_GLA_COLS + GATE_RANK) % F32_ROWS == 0
    n_conv = N_CONV_COLS // CONV_CAST_COLS
    n_gla = (N_GLA_COLS + LANES) // LANES

    def conv_block(*g):
        return jnp.minimum(step_of(*g), n_conv - 1)

    def gla_block(*g):
        return jnp.minimum(step_of(*g), n_gla - 1)

    def conv_window(*g):
        row = N_GLA_COLS + GATE_RANK + CONV_CAST_COLS * conv_block(*g)
        return (layer, pl.multiple_of(row, F32_ROWS), 0)

    in_specs = [
        pl.BlockSpec((pl.Element(1), pl.Element(CONV_CAST_COLS), pl.Element(D_MODEL)), conv_window),
        pl.BlockSpec((None, LANES, D_MODEL), lambda *g: (layer, gla_block(*g), 0)),
    ] + [pl.BlockSpec((None, square_rows, D_MODEL), lambda *g: (layer, square_block(*g), 0))] * 3
    out_specs = [
        pl.BlockSpec((D_MODEL, LANES), lambda *g: (0, gla_block(*g))),
        pl.BlockSpec((D_MODEL, CONV_CAST_COLS), lambda *g: (0, conv_block(*g))),
    ] + [pl.BlockSpec((square_rows, D_MODEL), lambda *g: (square_block(*g), 0))] * 3
    out_shape = [jax.ShapeDtypeStruct((D_MODEL, cols), jnp.bfloat16) for cols in _CAST_COLS]
    return in_specs, out_specs, out_shape


def _cast_layer(f32_mats, layer):
    square_rows = 64
    n_square = D_MODEL // square_rows
    in_specs, out_specs, out_shape = _cast_specs(
        layer, lambda i: i, square_rows, lambda i: jnp.minimum(i, n_square - 1))
    return pl.pallas_call(
        _cast_kernel,
        grid=(max(N_CAST_STEPS, n_square),),
        in_specs=in_specs,
        out_specs=out_specs,
        out_shape=out_shape,
        compiler_params=pltpu.CompilerParams(dimension_semantics=("arbitrary",)),
        name="cast_weights",
    )(*f32_mats)


def _layer_call(x, s0, tail0, small, big, final_g, layer, next_f32, *, tile, chunk,
                apply_final_norm):
    bsz, seq, _ = x.shape
    n_tiles = seq // tile
    gla_block = min(tile, GLA_BLOCK)
    assert seq % tile == 0 and tile % gla_block == 0 and gla_block % chunk == 0
    convert_next = next_f32 is not None
    kern = functools.partial(_layer_kernel, tile=tile, gla_block=gla_block, chunk=chunk,
                             apply_final_norm=apply_final_norm, convert_next=convert_next)
    tok_spec = pl.BlockSpec((None, tile, D_MODEL), lambda b, t: (b, t, 0))
    state_shape = (GLA_HEADS, GLA_DK, GLA_DV)
    in_specs = [
        tok_spec,
        _resident(state_shape),
        _resident((CONV_K - 1, D_MODEL)),
        _layer_slice(layer, (1, D_MODEL)),
        _layer_slice(layer, (GATE_RANK, GLA_K)),
        _layer_slice(layer, (1, GLA_K)),
        _layer_slice(layer, (1, GLA_V)),
        _layer_slice(layer, (CONV_K, D_MODEL)),
        _resident((1, D_MODEL)),
        _resident((D_MODEL, N_GLA_COLS + LANES)),
        _resident((D_MODEL, N_CONV_COLS)),
        _resident((GLA_V, D_MODEL)),
        _resident((D_MODEL, D_MODEL)),
        _resident((D_MODEL, D_MODEL)),
    ]
    out_specs = [
        tok_spec,
        pl.BlockSpec(state_shape, lambda b, t: (0, 0, 0)),
        pl.BlockSpec((CONV_K - 1, D_MODEL), lambda b, t: (0, 0)),
    ]
    out_shape = [
        jax.ShapeDtypeStruct(x.shape, jnp.float32),
        jax.ShapeDtypeStruct(state_shape, jnp.float32),
        jax.ShapeDtypeStruct((CONV_K - 1, D_MODEL), jnp.float32),
    ]
    args = [x, s0, tail0, *small, final_g, *big]
    if convert_next:
        steps = bsz * n_tiles
        assert D_MODEL % steps == 0
        assert steps >= N_CAST_STEPS

        def step_of(b, t):
            return b * n_tiles + t

        cast = _cast_specs(layer + 1, step_of, D_MODEL // steps, step_of)
        in_specs.extend(cast[0])
        out_specs.extend(cast[1])
        out_shape.extend(cast[2])
        args.extend(next_f32)
    return pl.pallas_call(
        kern,
        grid=(bsz, n_tiles),
        in_specs=in_specs,
        out_specs=out_specs,
        out_shape=out_shape,
        scratch_shapes=[
            pltpu.VMEM(state_shape, jnp.float32),
            pltpu.VMEM((CONV_K - 1, D_MODEL), jnp.float32),
        ],
        compiler_params=pltpu.CompilerParams(
            dimension_semantics=("arbitrary", "arbitrary"),
            vmem_limit_bytes=VMEM_LIMIT_BYTES),
        name="hybrid_layer",
    )(*args)


def kernel(x, meta, norm_g, w_in, w_gate_up, b_gate, gla_norm_g, w_o_gla, conv_w, w_o_conv, w_out,
           final_norm_g):
    depth = w_in.shape[0]
    small = (norm_g.reshape(depth, 1, D_MODEL), _bf16(w_gate_up), b_gate.reshape(depth, 1, GLA_K),
             gla_norm_g.reshape(depth, 1, GLA_V), conv_w)
    w_in_t = jnp.swapaxes(w_in, 1, 2)
    next_f32 = (w_in_t, w_in_t, w_o_gla, w_o_conv, w_out)
    big = _cast_layer(next_f32, 0)
    final_g = final_norm_g.reshape(1, D_MODEL)
    hm = meta.astype(x.dtype)[None]
    h = x
    zero_state = jnp.zeros((GLA_HEADS, GLA_DK, GLA_DV), jnp.float32)
    zero_tail = jnp.zeros((CONV_K - 1, D_MODEL), jnp.float32)
    for l in range(depth):
        last = l == depth - 1
        hm, s_meta, tail_meta = _layer_call(hm, zero_state, zero_tail, small, big, final_g, l, None,
                                            tile=N_META, chunk=N_META, apply_final_norm=False)
        h, _, _, *big = _layer_call(h, s_meta, tail_meta, small, big, final_g, l,
                                    None if last else next_f32,
                                    tile=TOKEN_TILE, chunk=CHUNK, apply_final_norm=last)
    return h
```

```python
import functools

import jax
import jax.numpy as jnp
from jax import lax
from jax.experimental import pallas as pl
from jax.experimental.pallas import tpu as pltpu

D_MODEL = 1024
N_META = 16
GLA_HEADS = 4
GLA_K = D_MODEL // 2
GLA_V = D_MODEL
GLA_DK = GLA_K // GLA_HEADS
GLA_DV = GLA_V // GLA_HEADS
GATE_RANK = 16
GATE_TAU = 16.0
CHUNK = 64
CONV_K = 3
EPS = 1e-6
LANES = 128
BF16_ROWS = 16

_GLA_SECTIONS = ("qk", "v", "r")
_CONV_SECTIONS = ("ch", "cb", "cc", "cz", "ga", "gb")
N_GLA_COLS = len(_GLA_SECTIONS) * D_MODEL
N_CONV_COLS = len(_CONV_SECTIONS) * D_MODEL

VMEM_LIMIT_BYTES = 60 * 1024 * 1024
TOKEN_TILE = 512
GLA_BLOCK = 256


def _sigmoid(z):
    return 1.0 / (1.0 + jnp.exp(-z))


def _silu(z):
    return z * _sigmoid(z)


def _log_sigmoid(z):
    return jnp.minimum(z, 0.0) - jnp.log(1.0 + jnp.exp(-jnp.abs(z)))


def _bf16(a):
    return a.astype(jnp.bfloat16)


def _dot(a, b):
    return jnp.dot(a, b, preferred_element_type=jnp.float32)


def _split2_bf16(a):
    hi = _bf16(a)
    lo = _bf16(a - hi.astype(jnp.float32))
    return hi, lo


def _gla_masks(block, chunk):
    ti = lax.broadcasted_iota(jnp.int32, (block, block), 0)
    si = lax.broadcasted_iota(jnp.int32, (block, block), 1)
    causal = ti >= si
    shift = chunk.bit_length() - 1
    same_chunk = lax.shift_right_logical(ti, shift) == lax.shift_right_logical(si, shift)
    return causal, _bf16((causal & same_chunk).astype(jnp.float32))


def _gla_prep(q, k, g, tril, chunk):
    block = q.shape[0]
    n_chunks = block // chunk
    g_hi, g_lo = _split2_bf16(g)
    b = _dot(tril, g_hi) + _dot(tril, g_lo)

    def rows(c):
        return slice(c * chunk, (c + 1) * chunk)

    b_last = [b[(c + 1) * chunk - 1:(c + 1) * chunk, :] for c in range(n_chunks)]
    base = [jnp.zeros((1, GLA_K), jnp.float32)]
    for c in range(n_chunks):
        base.append(base[c] + b_last[c])

    q_in = q * (jnp.exp(b) * (GLA_DK ** -0.5))
    k_in = k * jnp.exp(-b)
    q_in_c, q_dec_c, k_in_c, k_st_c, k_end_c = [], [], [], [], []
    for c in range(n_chunks):
        q_c = q_in[rows(c), :]
        q_in_c.append(_bf16(q_c))
        q_dec_c.append(_bf16(q_c * jnp.exp(base[c])))
        k_in_c.append(_bf16(k_in[rows(c), :]))
        k_st = k[rows(c), :] * jnp.exp(b_last[c] - b[rows(c), :])
        k_st_c.append(k_st)
        k_end_c.append(_bf16(k_st * jnp.exp(base[n_chunks] - base[c + 1])))
    q_dec = jnp.concatenate(q_dec_c, axis=0)
    k_end = jnp.concatenate(k_end_c, axis=0)
    block_decay = jnp.exp(base[n_chunks])

    zero_rows = jnp.zeros((chunk, GLA_K), jnp.bfloat16)
    k_seen = []
    for c in range(n_chunks):
        blocks = [_bf16(k_st_c[j] * jnp.exp(base[c] - base[j + 1])) for j in range(c)]
        blocks.append(k_in_c[c])
        blocks.extend([zero_rows] * (n_chunks - 1 - c))
        k_seen.append(jnp.concatenate(blocks, axis=0))
    return q_in_c, k_seen, q_dec, k_end, block_decay


def _gla_attend(operands, v, r, causal, gla_norm_g, state_ref):
    q_in_c, k_seen, q_dec, k_end, block_decay = operands
    n_chunks = len(q_in_c)
    nt_dims = (((1,), (1,)), ((), ()))
    o_heads = []
    for hd in range(GLA_HEADS):
        ks = slice(hd * GLA_DK, (hd + 1) * GLA_DK)
        vs = slice(hd * GLA_DV, (hd + 1) * GLA_DV)
        att = jnp.concatenate(
            [lax.dot_general(q_in_c[c][:, ks], k_seen[c][:, ks], nt_dims,
                             preferred_element_type=jnp.float32) for c in range(n_chunks)],
            axis=0)
        att = _bf16(jnp.where(causal, att, 0.0))
        st = state_ref[hd]
        oh = _dot(att, v[:, vs]) + _dot(q_dec[:, ks], _bf16(st))
        kv = lax.dot_general(k_end[:, ks], v[:, vs], (((0,), (0,)), ((), ())),
                             preferred_element_type=jnp.float32)
        decay_col = jnp.broadcast_to(block_decay[:, ks], (8, GLA_DK)).T[:, 0:1]
        state_ref[hd] = st * decay_col + kv
        oh = oh * lax.rsqrt(jnp.mean(oh * oh, axis=-1, keepdims=True) + EPS) * gla_norm_g[:, vs]
        o_heads.append(_bf16(oh * _silu(r[:, vs])))
    return jnp.concatenate(o_heads, axis=-1)


def _cast_w_in_blocks(f32_refs, bf16_refs):
    wt_conv_ref, wt_gla_ref = f32_refs[:2]
    o_gla_ref, o_conv_ref = bf16_refs[:2]
    o_conv_ref[...] = _bf16(wt_conv_ref[0].T)
    o_gla_ref[...] = _bf16(wt_gla_ref[...].T)


def _cast_square_rows(f32_refs, bf16_refs):
    for src_ref, dst_ref in zip(f32_refs[2:], bf16_refs[2:]):
        dst_ref[...] = _bf16(src_ref[...])


def _cast_kernel(*refs):
    _cast_w_in_blocks(refs[:N_CAST_IN], refs[N_CAST_IN:])
    _cast_square_rows(refs[:N_CAST_IN], refs[N_CAST_IN:])


def _layer_body(x, weights, state_ref, tail_ref, *, gla_block, chunk, cast_next=None):
    (norm_g_ref, w_gate_up_ref, b_gate_ref, gla_norm_g_ref, conv_w_ref,
     w_gla_ref, w_conv_ref, w_o_gla_ref, w_o_conv_ref, w_out_ref) = weights
    tile = x.shape[0]
    h = x * lax.rsqrt(jnp.mean(x * x, axis=-1, keepdims=True) + EPS) * norm_g_ref[...]
    h = _bf16(h)

    def proj(name):
        if name in _GLA_SECTIONS:
            w_ref, i = w_gla_ref, _GLA_SECTIONS.index(name)
        else:
            w_ref, i = w_conv_ref, _CONV_SECTIONS.index(name)
        return _dot(h, w_ref[:, i * D_MODEL:(i + 1) * D_MODEL])

    r_glr = _dot(h, w_gla_ref[:, 2 * D_MODEL:])
    r = r_glr[:, :D_MODEL]
    glr = r_glr[:, D_MODEL:D_MODEL + GATE_RANK]
    qk = proj("qk")
    z = _dot(_bf16(glr), w_gate_up_ref[...]) + b_gate_ref[...]
    g = _log_sigmoid(z) * (1.0 / GATE_TAU)
    blocks = [slice(s * gla_block, (s + 1) * gla_block) for s in range(tile // gla_block)]
    causal, tril = _gla_masks(gla_block, chunk)
    gla_norm_g = gla_norm_g_ref[...]

    def prep(blk):
        return _gla_prep(qk[blk, :GLA_K], qk[blk, GLA_K:], g[blk, :], tril, chunk)

    operands = prep(blocks[0])
    v = _bf16(proj("v"))
    gated = []
    for s, blk in enumerate(blocks):
        gated.append(_gla_attend(operands, v[blk, :], r[blk, :], causal, gla_norm_g, state_ref))
        if s + 1 < len(blocks):
            operands = prep(blocks[s + 1])
        if s == 0:
            cc = proj("cc")
            ch = proj("ch")
            cb = proj("cb")
    cz = proj("cz")
    if cast_next is not None:
        cast_next()
    y_gla = _dot(jnp.concatenate(gated, axis=0), w_o_gla_ref[...])

    u = cc * ch
    tail = tail_ref[...]
    tok = lax.broadcasted_iota(jnp.int32, (tile, D_MODEL), 0)
    u1 = jnp.where(tok == 0, tail[1:2, :], pltpu.roll(u, 1, 0))
    u2 = jnp.where(tok == 0, tail[0:1, :],
                   jnp.where(tok == 1, tail[1:2, :], pltpu.roll(u, 2, 0)))
    tail_ref[...] = u[tile - 2:tile, :]
    cw = conv_w_ref[...]
    y_c = cw[0:1, :] * u2 + cw[1:2, :] * u1 + cw[2:3, :] * u
    y_c = cb * y_c * _silu(cz)
    y_conv = _dot(_bf16(y_c), w_o_conv_ref[...])

    merged = _sigmoid(proj("ga")) * y_gla + _sigmoid(proj("gb")) * y_conv
    return x + _dot(_bf16(merged), w_out_ref[...])


N_LAYER_WEIGHTS = 10


def _layer_kernel(*refs, gla_block, chunk, apply_final_norm, convert_next):
    x_ref, meta_ref, final_g_ref = refs[:3]
    weights = refs[3:3 + N_LAYER_WEIGHTS]
    n_in = 3 + N_LAYER_WEIGHTS + (N_CAST_IN if convert_next else 0)
    cast_in = refs[3 + N_LAYER_WEIGHTS:n_in]
    y_ref, meta_out_ref = refs[n_in:n_in + 2]
    cast_out = refs[n_in + 2:-4]
    state_ref, tail_ref, meta_state_ref, meta_tail_ref = refs[-4:]
    t = pl.program_id(1)

    @pl.when(pl.program_id(0) * pl.num_programs(1) + t == 0)
    def _():
        state_ref[...] = jnp.zeros(state_ref.shape, jnp.float32)
        tail_ref[...] = jnp.zeros(tail_ref.shape, jnp.float32)
        meta_out_ref[...] = _layer_body(meta_ref[...], weights, state_ref, tail_ref,
                                        gla_block=N_META, chunk=N_META)
        meta_state_ref[...] = state_ref[...]
        meta_tail_ref[...] = tail_ref[...]

    @pl.when(t == 0)
    def _():
        state_ref[...] = meta_state_ref[...]
        tail_ref[...] = meta_tail_ref[...]

    def cast_next():
        _cast_square_rows(cast_in, cast_out)
        _cast_w_in_blocks(cast_in, cast_out)

    y = _layer_body(x_ref[...], weights, state_ref, tail_ref, gla_block=gla_block, chunk=chunk,
                    cast_next=cast_next if convert_next else None)
    if apply_final_norm:
        y = y * lax.rsqrt(jnp.mean(y * y, axis=-1, keepdims=True) + EPS) * final_g_ref[...]
    y_ref[...] = y


def _resident(shape):
    return pl.BlockSpec(shape, lambda b, t: (0,) * len(shape), pipeline_mode=pl.Buffered(1))


def _layer_slice(layer, shape):
    return pl.BlockSpec((None,) + shape, lambda b, t: (layer,) + (0,) * len(shape),
                        pipeline_mode=pl.Buffered(1))


N_CAST_IN = 5
_CAST_COLS = (N_GLA_COLS + LANES, N_CONV_COLS, D_MODEL, D_MODEL, D_MODEL)
CONV_CAST_COLS = 256
N_CAST_STEPS = max(N_CONV_COLS // CONV_CAST_COLS, (N_GLA_COLS + LANES) // LANES)
F32_ROWS = 8


def _cast_specs(layer, step_of, square_rows, square_block):
    assert square_rows % BF16_ROWS == 0 and (N_GLA_COLS + GATE_RANK) % F32_ROWS == 0
    n_conv = N_CONV_COLS // CONV_CAST_COLS
    n_gla = (N_GLA_COLS + LANES) // LANES

    def conv_block(*g):
        return jnp.minimum(step_of(*g), n_conv - 1)

    def gla_block(*g):
        return jnp.minimum(step_of(*g), n_gla - 1)

    def conv_window(*g):
        row = N_GLA_COLS + GATE_RANK + CONV_CAST_COLS * conv_block(*g)
        return (layer, pl.multiple_of(row, F32_ROWS), 0)

    in_specs = [
        pl.BlockSpec((pl.Element(1), pl.Element(CONV_CAST_COLS), pl.Element(D_MODEL)), conv_window),
        pl.BlockSpec((None, LANES, D_MODEL), lambda *g: (layer, gla_block(*g), 0)),
    ] + [pl.BlockSpec((None, square_rows, D_MODEL), lambda *g: (layer, square_block(*g), 0))] * 3
    out_specs = [
        pl.BlockSpec((D_MODEL, LANES), lambda *g: (0, gla_block(*g))),
        pl.BlockSpec((D_MODEL, CONV_CAST_COLS), lambda *g: (0, conv_block(*g))),
    ] + [pl.BlockSpec((square_rows, D_MODEL), lambda *g: (square_block(*g), 0))] * 3
    out_shape = [jax.ShapeDtypeStruct((D_MODEL, cols), jnp.bfloat16) for cols in _CAST_COLS]
    return in_specs, out_specs, out_shape


def _cast_layer(f32_mats, layer):
    square_rows = 64
    n_square = D_MODEL // square_rows
    in_specs, out_specs, out_shape = _cast_specs(
        layer, lambda i: i, square_rows, lambda i: jnp.minimum(i, n_square - 1))
    return pl.pallas_call(
        _cast_kernel,
        grid=(max(N_CAST_STEPS, n_square),),
        in_specs=in_specs,
        out_specs=out_specs,
        out_shape=out_shape,
        compiler_params=pltpu.CompilerParams(dimension_semantics=("arbitrary",)),
        name="cast_weights",
    )(*f32_mats)


def _layer_call(x, meta_h, small, big, final_g, layer, next_f32, *, tile, chunk,
                apply_final_norm):
    bsz, seq, _ = x.shape
    n_tiles = seq // tile
    gla_block = min(tile, GLA_BLOCK)
    assert seq % tile == 0 and tile % gla_block == 0 and gla_block % chunk == 0
    convert_next = next_f32 is not None
    kern = functools.partial(_layer_kernel, gla_block=gla_block, chunk=chunk,
                             apply_final_norm=apply_final_norm, convert_next=convert_next)
    tok_spec = pl.BlockSpec((None, tile, D_MODEL), lambda b, t: (b, t, 0))
    state_shape = (GLA_HEADS, GLA_DK, GLA_DV)
    tail_shape = (CONV_K - 1, D_MODEL)
    in_specs = [
        tok_spec,
        _resident((N_META, D_MODEL)),
        _resident((1, D_MODEL)),
        _layer_slice(layer, (1, D_MODEL)),
        _layer_slice(layer, (GATE_RANK, GLA_K)),
        _layer_slice(layer, (1, GLA_K)),
        _layer_slice(layer, (1, GLA_V)),
        _layer_slice(layer, (CONV_K, D_MODEL)),
        _resident((D_MODEL, N_GLA_COLS + LANES)),
        _resident((D_MODEL, N_CONV_COLS)),
        _resident((GLA_V, D_MODEL)),
        _resident((D_MODEL, D_MODEL)),
        _resident((D_MODEL, D_MODEL)),
    ]
    out_specs = [tok_spec, pl.BlockSpec((N_META, D_MODEL), lambda b, t: (0, 0))]
    out_shape = [jax.ShapeDtypeStruct(x.shape, jnp.float32),
                 jax.ShapeDtypeStruct((N_META, D_MODEL), jnp.float32)]
    args = [x, meta_h, final_g, *small, *big]
    if convert_next:
        steps = bsz * n_tiles
        assert D_MODEL % steps == 0
        assert steps >= N_CAST_STEPS

        def step_of(b, t):
            return b * n_tiles + t

        cast = _cast_specs(layer + 1, step_of, D_MODEL // steps, step_of)
        in_specs.extend(cast[0])
        out_specs.extend(cast[1])
        out_shape.extend(cast[2])
        args.extend(next_f32)
    return pl.pallas_call(
        kern,
        grid=(bsz, n_tiles),
        in_specs=in_specs,
        out_specs=out_specs,
        out_shape=out_shape,
        scratch_shapes=[
            pltpu.VMEM(state_shape, jnp.float32),
            pltpu.VMEM(tail_shape, jnp.float32),
            pltpu.VMEM(state_shape, jnp.float32),
            pltpu.VMEM(tail_shape, jnp.float32),
        ],
        compiler_params=pltpu.CompilerParams(
            dimension_semantics=("arbitrary", "arbitrary"),
            vmem_limit_bytes=VMEM_LIMIT_BYTES),
        name="hybrid_layer",
    )(*args)


def kernel(x, meta, norm_g, w_in, w_gate_up, b_gate, gla_norm_g, w_o_gla, conv_w, w_o_conv, w_out,
           final_norm_g):
    depth = w_in.shape[0]
    small = (norm_g.reshape(depth, 1, D_MODEL), _bf16(w_gate_up), b_gate.reshape(depth, 1, GLA_K),
             gla_norm_g.reshape(depth, 1, GLA_V), conv_w)
    w_in_t = jnp.swapaxes(w_in, 1, 2)
    next_f32 = (w_in_t, w_in_t, w_o_gla, w_o_conv, w_out)
    big = _cast_layer(next_f32, 0)
    final_g = final_norm_g.reshape(1, D_MODEL)
    meta_h = meta.astype(x.dtype)
    h = x
    for l in range(depth):
        last = l == depth - 1
        h, meta_h, *big = _layer_call(h, meta_h, small, big, final_g, l, None if last else next_f32,
                                      tile=TOKEN_TILE, chunk=CHUNK, apply_final_norm=last)
    return h
```

```python
import functools

import jax
import jax.numpy as jnp
from jax import lax
from jax.experimental import pallas as pl
from jax.experimental.pallas import tpu as pltpu

D_MODEL = 1024
N_META = 16
GLA_HEADS = 4
GLA_K = D_MODEL // 2
GLA_V = D_MODEL
GLA_DK = GLA_K // GLA_HEADS
GLA_DV = GLA_V // GLA_HEADS
GATE_RANK = 16
GATE_TAU = 16.0
CHUNK = 64
CONV_K = 3
EPS = 1e-6
LANES = 128
BF16_ROWS = 16

_GLA_SECTIONS = ("qk", "v", "r")
_CONV_SECTIONS = ("ch", "cb", "cc", "cz", "ga", "gb")
N_GLA_COLS = len(_GLA_SECTIONS) * D_MODEL
N_CONV_COLS = len(_CONV_SECTIONS) * D_MODEL

VMEM_LIMIT_BYTES = 60 * 1024 * 1024
TOKEN_TILE = 512
GLA_BLOCK = 256


def _sigmoid(z):
    return 0.5 * jnp.tanh(0.5 * z) + 0.5


def _silu(z):
    return z * _sigmoid(z)


def _log_sigmoid(z):
    return jnp.minimum(z, 0.0) - jnp.log(1.0 + jnp.exp(-jnp.abs(z)))


def _bf16(a):
    return a.astype(jnp.bfloat16)


def _dot(a, b):
    return jnp.dot(a, b, preferred_element_type=jnp.float32)


def _split2_bf16(a):
    hi = _bf16(a)
    lo = _bf16(a - hi.astype(jnp.float32))
    return hi, lo


def _gla_masks(block, chunk):
    ti = lax.broadcasted_iota(jnp.int32, (block, block), 0)
    si = lax.broadcasted_iota(jnp.int32, (block, block), 1)
    causal = ti >= si
    shift = chunk.bit_length() - 1
    same_chunk = lax.shift_right_logical(ti, shift) == lax.shift_right_logical(si, shift)
    return causal, _bf16((causal & same_chunk).astype(jnp.float32))


def _gla_prep(q, k, g, tril, chunk):
    block = q.shape[0]
    n_chunks = block // chunk
    g_hi, g_lo = _split2_bf16(g)
    b = _dot(tril, g_hi) + _dot(tril, g_lo)

    def rows(c):
        return slice(c * chunk, (c + 1) * chunk)

    b_last = [b[(c + 1) * chunk - 1:(c + 1) * chunk, :] for c in range(n_chunks)]
    base = [jnp.zeros((1, GLA_K), jnp.float32)]
    for c in range(n_chunks):
        base.append(base[c] + b_last[c])

    q_in = q * (jnp.exp(b) * (GLA_DK ** -0.5))
    k_in = k * jnp.exp(-b)
    q_in_c, q_dec_c, k_in_c, k_st_c, k_end_c = [], [], [], [], []
    for c in range(n_chunks):
        q_c = q_in[rows(c), :]
        q_in_c.append(_bf16(q_c))
        q_dec_c.append(_bf16(q_c * jnp.exp(base[c])))
        k_in_c.append(_bf16(k_in[rows(c), :]))
        k_st = k[rows(c), :] * jnp.exp(b_last[c] - b[rows(c), :])
        k_st_c.append(k_st)
        k_end_c.append(_bf16(k_st * jnp.exp(base[n_chunks] - base[c + 1])))
    q_dec = jnp.concatenate(q_dec_c, axis=0)
    k_end = jnp.concatenate(k_end_c, axis=0)
    block_decay = jnp.exp(base[n_chunks])

    zero_rows = jnp.zeros((chunk, GLA_K), jnp.bfloat16)
    k_seen = []
    for c in range(n_chunks):
        blocks = [_bf16(k_st_c[j] * jnp.exp(base[c] - base[j + 1])) for j in range(c)]
        blocks.append(k_in_c[c])
        blocks.extend([zero_rows] * (n_chunks - 1 - c))
        k_seen.append(jnp.concatenate(blocks, axis=0))
    return q_in_c, k_seen, q_dec, k_end, block_decay


def _gla_attend(operands, v, r, causal, gla_norm_g, state_ref):
    q_in_c, k_seen, q_dec, k_end, block_decay = operands
    n_chunks = len(q_in_c)
    nt_dims = (((1,), (1,)), ((), ()))
    o_heads = []
    for hd in range(GLA_HEADS):
        ks = slice(hd * GLA_DK, (hd + 1) * GLA_DK)
        vs = slice(hd * GLA_DV, (hd + 1) * GLA_DV)
        att = jnp.concatenate(
            [lax.dot_general(q_in_c[c][:, ks], k_seen[c][:, ks], nt_dims,
                             preferred_element_type=jnp.float32) for c in range(n_chunks)],
            axis=0)
        att = _bf16(jnp.where(causal, att, 0.0))
        st = state_ref[hd]
        oh = _dot(att, v[:, vs]) + _dot(q_dec[:, ks], _bf16(st))
        kv = lax.dot_general(k_end[:, ks], v[:, vs], (((0,), (0,)), ((), ())),
                             preferred_element_type=jnp.float32)
        decay_col = jnp.broadcast_to(block_decay[:, ks], (8, GLA_DK)).T[:, 0:1]
        state_ref[hd] = st * decay_col + kv
        oh = oh * lax.rsqrt(jnp.mean(oh * oh, axis=-1, keepdims=True) + EPS) * gla_norm_g[:, vs]
        o_heads.append(_bf16(oh * _silu(r[:, vs])))
    return jnp.concatenate(o_heads, axis=-1)


def _cast_w_in_blocks(f32_refs, bf16_refs):
    wt_conv_ref, wt_gla_ref = f32_refs[:2]
    o_gla_ref, o_conv_ref = bf16_refs[:2]
    o_conv_ref[...] = _bf16(wt_conv_ref[0].T)
    o_gla_ref[...] = _bf16(wt_gla_ref[...].T)


def _cast_square_rows(f32_refs, bf16_refs):
    for src_ref, dst_ref in zip(f32_refs[2:], bf16_refs[2:]):
        dst_ref[...] = _bf16(src_ref[...])


def _cast_kernel(*refs):
    _cast_w_in_blocks(refs[:N_CAST_IN], refs[N_CAST_IN:])
    _cast_square_rows(refs[:N_CAST_IN], refs[N_CAST_IN:])


def _layer_body(x, weights, state_ref, tail_ref, *, gla_block, chunk, cast_next=None):
    (norm_g_ref, w_gate_up_ref, b_gate_ref, gla_norm_g_ref, conv_w_ref,
     w_gla_ref, w_conv_ref, w_o_gla_ref, w_o_conv_ref, w_out_ref) = weights
    tile = x.shape[0]
    h = x * lax.rsqrt(jnp.mean(x * x, axis=-1, keepdims=True) + EPS) * norm_g_ref[...]
    h = _bf16(h)

    def proj(name):
        if name in _GLA_SECTIONS:
            w_ref, i = w_gla_ref, _GLA_SECTIONS.index(name)
        else:
            w_ref, i = w_conv_ref, _CONV_SECTIONS.index(name)
        return _dot(h, w_ref[:, i * D_MODEL:(i + 1) * D_MODEL])

    r_glr = _dot(h, w_gla_ref[:, 2 * D_MODEL:])
    r = r_glr[:, :D_MODEL]
    glr = r_glr[:, D_MODEL:D_MODEL + GATE_RANK]
    qk = proj("qk")
    z = _dot(_bf16(glr), w_gate_up_ref[...]) + b_gate_ref[...]
    g = _log_sigmoid(z) * (1.0 / GATE_TAU)
    blocks = [slice(s * gla_block, (s + 1) * gla_block) for s in range(tile // gla_block)]
    causal, tril = _gla_masks(gla_block, chunk)
    gla_norm_g = gla_norm_g_ref[...]

    def prep(blk):
        return _gla_prep(qk[blk, :GLA_K], qk[blk, GLA_K:], g[blk, :], tril, chunk)

    operands = prep(blocks[0])
    v = _bf16(proj("v"))
    gated = []
    for s, blk in enumerate(blocks):
        gated.append(_gla_attend(operands, v[blk, :], r[blk, :], causal, gla_norm_g, state_ref))
        if s + 1 < len(blocks):
            operands = prep(blocks[s + 1])
        if s == 0:
            cc = proj("cc")
            ch = proj("ch")
            cb = proj("cb")
    cz = proj("cz")
    if cast_next is not None:
        cast_next()
    y_gla = _dot(jnp.concatenate(gated, axis=0), w_o_gla_ref[...])

    merge_a = _sigmoid(proj("ga"))
    merge_b = _sigmoid(proj("gb"))

    u = cc * ch
    tail = tail_ref[...]
    tok = lax.broadcasted_iota(jnp.int32, (tile, D_MODEL), 0)
    u1 = jnp.where(tok == 0, tail[1:2, :], pltpu.roll(u, 1, 0))
    u2 = jnp.where(tok == 0, tail[0:1, :],
                   jnp.where(tok == 1, tail[1:2, :], pltpu.roll(u, 2, 0)))
    tail_ref[...] = u[tile - 2:tile, :]
    cw = conv_w_ref[...]
    y_c = cw[0:1, :] * u2 + cw[1:2, :] * u1 + cw[2:3, :] * u
    y_c = cb * y_c * _silu(cz)
    y_conv = _dot(_bf16(y_c), w_o_conv_ref[...])

    merged = merge_a * y_gla + merge_b * y_conv
    return x + _dot(_bf16(merged), w_out_ref[...])


N_LAYER_WEIGHTS = 10


def _layer_kernel(*refs, gla_block, chunk, apply_final_norm, convert_next):
    x_ref, meta_ref, final_g_ref = refs[:3]
    weights = refs[3:3 + N_LAYER_WEIGHTS]
    n_in = 3 + N_LAYER_WEIGHTS + (N_CAST_IN if convert_next else 0)
    cast_in = refs[3 + N_LAYER_WEIGHTS:n_in]
    y_ref, meta_out_ref = refs[n_in:n_in + 2]
    cast_out = refs[n_in + 2:-4]
    state_ref, tail_ref, meta_state_ref, meta_tail_ref = refs[-4:]
    t = pl.program_id(1)

    @pl.when(pl.program_id(0) * pl.num_programs(1) + t == 0)
    def _():
        state_ref[...] = jnp.zeros(state_ref.shape, jnp.float32)
        tail_ref[...] = jnp.zeros(tail_ref.shape, jnp.float32)
        meta_out_ref[...] = _layer_body(meta_ref[...], weights, state_ref, tail_ref,
                                        gla_block=N_META, chunk=N_META)
        meta_state_ref[...] = state_ref[...]
        meta_tail_ref[...] = tail_ref[...]

    @pl.when(t == 0)
    def _():
        state_ref[...] = meta_state_ref[...]
        tail_ref[...] = meta_tail_ref[...]

    def cast_next():
        _cast_square_rows(cast_in, cast_out)
        _cast_w_in_blocks(cast_in, cast_out)

    y = _layer_body(x_ref[...], weights, state_ref, tail_ref, gla_block=gla_block, chunk=chunk,
                    cast_next=cast_next if convert_next else None)
    if apply_final_norm:
        y = y * lax.rsqrt(jnp.mean(y * y, axis=-1, keepdims=True) + EPS) * final_g_ref[...]
    y_ref[...] = y


def _resident(shape):
    return pl.BlockSpec(shape, lambda b, t: (0,) * len(shape), pipeline_mode=pl.Buffered(1))


def _layer_slice(layer, shape):
    return pl.BlockSpec((None,) + shape, lambda b, t: (layer,) + (0,) * len(shape),
                        pipeline_mode=pl.Buffered(1))


N_CAST_IN = 5
_CAST_COLS = (N_GLA_COLS + LANES, N_CONV_COLS, D_MODEL, D_MODEL, D_MODEL)
F32_ROWS = 8
_CAST_BLOCKS_INLINE = (256, LANES)
_CAST_BLOCKS_ALONE = (1024, 640, 256)


def _cast_steps(conv_cols, gla_cols):
    assert N_CONV_COLS % conv_cols == 0 and (N_GLA_COLS + LANES) % gla_cols == 0
    return max(N_CONV_COLS // conv_cols, (N_GLA_COLS + LANES) // gla_cols)


def _cast_specs(layer, step_of, conv_cols, gla_cols, square_rows, square_block):
    assert square_rows % BF16_ROWS == 0 and (N_GLA_COLS + GATE_RANK) % F32_ROWS == 0
    n_conv = N_CONV_COLS // conv_cols
    n_gla = (N_GLA_COLS + LANES) // gla_cols

    def conv_block(*g):
        return jnp.minimum(step_of(*g), n_conv - 1)

    def gla_block(*g):
        return jnp.minimum(step_of(*g), n_gla - 1)

    def conv_window(*g):
        row = N_GLA_COLS + GATE_RANK + conv_cols * conv_block(*g)
        return (layer, pl.multiple_of(row, F32_ROWS), 0)

    in_specs = [
        pl.BlockSpec((pl.Element(1), pl.Element(conv_cols), pl.Element(D_MODEL)), conv_window),
        pl.BlockSpec((None, gla_cols, D_MODEL), lambda *g: (layer, gla_block(*g), 0)),
    ] + [pl.BlockSpec((None, square_rows, D_MODEL), lambda *g: (layer, square_block(*g), 0))] * 3
    out_specs = [
        pl.BlockSpec((D_MODEL, gla_cols), lambda *g: (0, gla_block(*g))),
        pl.BlockSpec((D_MODEL, conv_cols), lambda *g: (0, conv_block(*g))),
    ] + [pl.BlockSpec((square_rows, D_MODEL), lambda *g: (square_block(*g), 0))] * 3
    out_shape = [jax.ShapeDtypeStruct((D_MODEL, cols), jnp.bfloat16) for cols in _CAST_COLS]
    return in_specs, out_specs, out_shape


def _cast_layer(f32_mats, layer):
    conv_cols, gla_cols, square_rows = _CAST_BLOCKS_ALONE
    n_square = D_MODEL // square_rows
    in_specs, out_specs, out_shape = _cast_specs(
        layer, lambda i: i, conv_cols, gla_cols, square_rows,
        lambda i: jnp.minimum(i, n_square - 1))
    return pl.pallas_call(
        _cast_kernel,
        grid=(max(_cast_steps(conv_cols, gla_cols), n_square),),
        in_specs=in_specs,
        out_specs=out_specs,
        out_shape=out_shape,
        compiler_params=pltpu.CompilerParams(dimension_semantics=("arbitrary",),
                                             vmem_limit_bytes=VMEM_LIMIT_BYTES),
        name="cast_weights",
    )(*f32_mats)


def _layer_call(x, meta_h, small, big, final_g, layer, next_f32, *, tile, chunk,
                apply_final_norm):
    bsz, seq, _ = x.shape
    n_tiles = seq // tile
    gla_block = min(tile, GLA_BLOCK)
    assert seq % tile == 0 and tile % gla_block == 0 and gla_block % chunk == 0
    convert_next = next_f32 is not None
    kern = functools.partial(_layer_kernel, gla_block=gla_block, chunk=chunk,
                             apply_final_norm=apply_final_norm, convert_next=convert_next)
    tok_spec = pl.BlockSpec((None, tile, D_MODEL), lambda b, t: (b, t, 0))
    state_shape = (GLA_HEADS, GLA_DK, GLA_DV)
    tail_shape = (CONV_K - 1, D_MODEL)
    in_specs = [
        tok_spec,
        _resident((N_META, D_MODEL)),
        _resident((1, D_MODEL)),
        _layer_slice(layer, (1, D_MODEL)),
        _layer_slice(layer, (GATE_RANK, GLA_K)),
        _layer_slice(layer, (1, GLA_K)),
        _layer_slice(layer, (1, GLA_V)),
        _layer_slice(layer, (CONV_K, D_MODEL)),
        _resident((D_MODEL, N_GLA_COLS + LANES)),
        _resident((D_MODEL, N_CONV_COLS)),
        _resident((GLA_V, D_MODEL)),
        _resident((D_MODEL, D_MODEL)),
        _resident((D_MODEL, D_MODEL)),
    ]
    out_specs = [tok_spec, pl.BlockSpec((N_META, D_MODEL), lambda b, t: (0, 0))]
    out_shape = [jax.ShapeDtypeStruct(x.shape, jnp.float32),
                 jax.ShapeDtypeStruct((N_META, D_MODEL), jnp.float32)]
    args = [x, meta_h, final_g, *small, *big]
    if convert_next:
        steps = bsz * n_tiles
        assert D_MODEL % steps == 0
        assert steps >= _cast_steps(*_CAST_BLOCKS_INLINE)

        def step_of(b, t):
            return b * n_tiles + t

        cast = _cast_specs(layer + 1, step_of, *_CAST_BLOCKS_INLINE, D_MODEL // steps, step_of)
        in_specs.extend(cast[0])
        out_specs.extend(cast[1])
        out_shape.extend(cast[2])
        args.extend(next_f32)
    return pl.pallas_call(
        kern,
        grid=(bsz, n_tiles),
        in_specs=in_specs,
        out_specs=out_specs,
        out_shape=out_shape,
        scratch_shapes=[
            pltpu.VMEM(state_shape, jnp.float32),
            pltpu.VMEM(tail_shape, jnp.float32),
            pltpu.VMEM(state_shape, jnp.float32),
            pltpu.VMEM(tail_shape, jnp.float32),
        ],
        compiler_params=pltpu.CompilerParams(
            dimension_semantics=("arbitrary", "arbitrary"),
            vmem_limit_bytes=VMEM_LIMIT_BYTES),
        name="hybrid_layer",
    )(*args)


def kernel(x, meta, norm_g, w_in, w_gate_up, b_gate, gla_norm_g, w_o_gla, conv_w, w_o_conv, w_out,
           final_norm_g):
    depth = w_in.shape[0]
    small = (norm_g.reshape(depth, 1, D_MODEL), _bf16(w_gate_up), b_gate.reshape(depth, 1, GLA_K),
             gla_norm_g.reshape(depth, 1, GLA_V), conv_w)
    w_in_t = jnp.swapaxes(w_in, 1, 2)
    next_f32 = (w_in_t, w_in_t, w_o_gla, w_o_conv, w_out)
    big = _cast_layer(next_f32, 0)
    final_g = final_norm_g.reshape(1, D_MODEL)
    meta_h = meta.astype(x.dtype)
    h = x
    for l in range(depth):
        last = l == depth - 1
        h, meta_h, *big = _layer_call(h, meta_h, small, big, final_g, l, None if last else next_f32,
                                      tile=TOKEN_TILE, chunk=CHUNK, apply_final_norm=last)
    return h
```

```python
import functools

import jax
import jax.numpy as jnp
from jax import lax
from jax.experimental import pallas as pl
from jax.experimental.pallas import tpu as pltpu

D_MODEL = 1024
N_META = 16
GLA_HEADS = 4
GLA_K = D_MODEL // 2
GLA_V = D_MODEL
GLA_DK = GLA_K // GLA_HEADS
GLA_DV = GLA_V // GLA_HEADS
GATE_RANK = 16
GATE_TAU = 16.0
CHUNK = 64
CONV_K = 3
EPS = 1e-6
LANES = 128
BF16_ROWS = 16

_GLA_SECTIONS = ("qk", "v", "r")
_CONV_SECTIONS = ("ch", "cb", "cc", "cz", "ga", "gb")
N_GLA_COLS = len(_GLA_SECTIONS) * D_MODEL
N_CONV_COLS = len(_CONV_SECTIONS) * D_MODEL

VMEM_LIMIT_BYTES = 60 * 1024 * 1024
TOKEN_TILE = 512
GLA_BLOCK = 256


def _sigmoid(z):
    return 0.5 * jnp.tanh(0.5 * z) + 0.5


def _silu(z):
    return z * _sigmoid(z)


def _log_sigmoid(z):
    return jnp.minimum(z, 0.0) - jnp.log(1.0 + jnp.exp(-jnp.abs(z)))


def _bf16(a):
    return a.astype(jnp.bfloat16)


def _dot(a, b):
    return jnp.dot(a, b, preferred_element_type=jnp.float32)


def _split2_bf16(a):
    hi = _bf16(a)
    lo = _bf16(a - hi.astype(jnp.float32))
    return hi, lo


def _gla_masks(block, chunk):
    ti = lax.broadcasted_iota(jnp.int32, (block, block), 0)
    si = lax.broadcasted_iota(jnp.int32, (block, block), 1)
    causal = ti >= si
    shift = chunk.bit_length() - 1
    same_chunk = lax.shift_right_logical(ti, shift) == lax.shift_right_logical(si, shift)
    return causal, _bf16((causal & same_chunk).astype(jnp.float32))


def _gla_prep(q, k, g, tril, chunk):
    block = q.shape[0]
    n_chunks = block // chunk
    g_hi, g_lo = _split2_bf16(g)
    b = _dot(tril, g_hi) + _dot(tril, g_lo)

    def rows(c):
        return slice(c * chunk, (c + 1) * chunk)

    b_last = [b[(c + 1) * chunk - 1:(c + 1) * chunk, :] for c in range(n_chunks)]
    base = [jnp.zeros((1, GLA_K), jnp.float32)]
    for c in range(n_chunks):
        base.append(base[c] + b_last[c])

    q_in = q * (jnp.exp(b) * (GLA_DK ** -0.5))
    k_in = k * jnp.exp(-b)
    q_in_c, q_dec_c, k_in_c, k_st_c, k_end_c = [], [], [], [], []
    for c in range(n_chunks):
        q_c = q_in[rows(c), :]
        q_in_c.append(_bf16(q_c))
        q_dec_c.append(_bf16(q_c * jnp.exp(base[c])))
        k_in_c.append(_bf16(k_in[rows(c), :]))
        k_st = k[rows(c), :] * jnp.exp(b_last[c] - b[rows(c), :])
        k_st_c.append(k_st)
        k_end_c.append(_bf16(k_st * jnp.exp(base[n_chunks] - base[c + 1])))
    q_dec = jnp.concatenate(q_dec_c, axis=0)
    k_end = jnp.concatenate(k_end_c, axis=0)
    block_decay = jnp.exp(base[n_chunks])

    zero_rows = jnp.zeros((chunk, GLA_K), jnp.bfloat16)
    k_seen = []
    for c in range(n_chunks):
        blocks = [_bf16(k_st_c[j] * jnp.exp(base[c] - base[j + 1])) for j in range(c)]
        blocks.append(k_in_c[c])
        blocks.extend([zero_rows] * (n_chunks - 1 - c))
        k_seen.append(jnp.concatenate(blocks, axis=0))
    return q_in_c, k_seen, q_dec, k_end, block_decay


def _gla_attend(operands, v, r, causal, gla_norm_g, state_ref):
    q_in_c, k_seen, q_dec, k_end, block_decay = operands
    n_chunks = len(q_in_c)
    nt_dims = (((1,), (1,)), ((), ()))
    o_heads = []
    for hd in range(GLA_HEADS):
        ks = slice(hd * GLA_DK, (hd + 1) * GLA_DK)
        vs = slice(hd * GLA_DV, (hd + 1) * GLA_DV)
        att = jnp.concatenate(
            [lax.dot_general(q_in_c[c][:, ks], k_seen[c][:, ks], nt_dims,
                             preferred_element_type=jnp.float32) for c in range(n_chunks)],
            axis=0)
        att = _bf16(jnp.where(causal, att, 0.0))
        st = state_ref[hd]
        oh = _dot(att, v[:, vs]) + _dot(q_dec[:, ks], _bf16(st))
        kv = lax.dot_general(k_end[:, ks], v[:, vs], (((0,), (0,)), ((), ())),
                             preferred_element_type=jnp.float32)
        decay_col = jnp.broadcast_to(block_decay[:, ks], (8, GLA_DK)).T[:, 0:1]
        state_ref[hd] = st * decay_col + kv
        oh = oh * lax.rsqrt(jnp.mean(oh * oh, axis=-1, keepdims=True) + EPS) * gla_norm_g[:, vs]
        o_heads.append(_bf16(oh * _silu(r[:, vs])))
    return jnp.concatenate(o_heads, axis=-1)


def _cast_w_in_blocks(f32_refs, bf16_refs):
    wt_conv_ref, wt_gla_ref = f32_refs[:2]
    o_gla_ref, o_conv_ref = bf16_refs[:2]
    o_conv_ref[...] = _bf16(wt_conv_ref[0].T)
    o_gla_ref[...] = _bf16(wt_gla_ref[...].T)


def _cast_square_rows(f32_refs, bf16_refs):
    for src_ref, dst_ref in zip(f32_refs[2:], bf16_refs[2:]):
        dst_ref[...] = _bf16(src_ref[...])


def _cast_kernel(*refs):
    _cast_w_in_blocks(refs[:N_CAST_IN], refs[N_CAST_IN:])
    _cast_square_rows(refs[:N_CAST_IN], refs[N_CAST_IN:])


_VEC_ROWS = {"norm_g": (0, 1), "gla_norm_g": (1, 1), "conv_w": (2, CONV_K), "final_g": (5, 1),
             "b_gate": (6, 1)}
N_VEC_ROWS = 8


def _vec(vec_ref, name):
    first, count = _VEC_ROWS[name]
    return vec_ref[first:first + count, :]


def _layer_body(x, weights, state_ref, tail_ref, *, gla_block, chunk, cast_next=None):
    vec_ref, w_gate_up_ref, w_gla_ref, w_conv_ref, w_o_gla_ref, w_o_conv_ref, w_out_ref = weights
    tile = x.shape[0]
    h = x * lax.rsqrt(jnp.mean(x * x, axis=-1, keepdims=True) + EPS) * _vec(vec_ref, "norm_g")
    h = _bf16(h)

    def proj(name):
        if name in _GLA_SECTIONS:
            w_ref, i = w_gla_ref, _GLA_SECTIONS.index(name)
        else:
            w_ref, i = w_conv_ref, _CONV_SECTIONS.index(name)
        return _dot(h, w_ref[:, i * D_MODEL:(i + 1) * D_MODEL])

    r_glr = _dot(h, w_gla_ref[:, 2 * D_MODEL:])
    r = r_glr[:, :D_MODEL]
    glr = r_glr[:, D_MODEL:D_MODEL + GATE_RANK]
    qk = proj("qk")
    z = _dot(_bf16(glr), w_gate_up_ref[...]) + _vec(vec_ref, "b_gate")[:, :GLA_K]
    g = _log_sigmoid(z) * (1.0 / GATE_TAU)
    blocks = [slice(s * gla_block, (s + 1) * gla_block) for s in range(tile // gla_block)]
    causal, tril = _gla_masks(gla_block, chunk)
    gla_norm_g = _vec(vec_ref, "gla_norm_g")

    def prep(blk):
        return _gla_prep(qk[blk, :GLA_K], qk[blk, GLA_K:], g[blk, :], tril, chunk)

    operands = prep(blocks[0])
    v = _bf16(proj("v"))
    gated = []
    for s, blk in enumerate(blocks):
        gated.append(_gla_attend(operands, v[blk, :], r[blk, :], causal, gla_norm_g, state_ref))
        if s + 1 < len(blocks):
            operands = prep(blocks[s + 1])
        if s == 0:
            cc = proj("cc")
            ch = proj("ch")
            cb = proj("cb")
    cz = proj("cz")
    if cast_next is not None:
        cast_next()
    y_gla = _dot(jnp.concatenate(gated, axis=0), w_o_gla_ref[...])

    merge_a = _sigmoid(proj("ga"))
    merge_b = _sigmoid(proj("gb"))

    u = cc * ch
    tail = tail_ref[...]
    tok = lax.broadcasted_iota(jnp.int32, (tile, D_MODEL), 0)
    u1 = jnp.where(tok == 0, tail[1:2, :], pltpu.roll(u, 1, 0))
    u2 = jnp.where(tok == 0, tail[0:1, :],
                   jnp.where(tok == 1, tail[1:2, :], pltpu.roll(u, 2, 0)))
    tail_ref[...] = u[tile - 2:tile, :]
    cw = _vec(vec_ref, "conv_w")
    y_c = cw[0:1, :] * u2 + cw[1:2, :] * u1 + cw[2:3, :] * u
    y_c = cb * y_c * _silu(cz)
    y_conv = _dot(_bf16(y_c), w_o_conv_ref[...])

    merged = merge_a * y_gla + merge_b * y_conv
    return x + _dot(_bf16(merged), w_out_ref[...])


N_LAYER_WEIGHTS = 7


def _layer_kernel(*refs, gla_block, chunk, apply_final_norm, convert_next):
    x_ref, meta_ref = refs[:2]
    weights = refs[2:2 + N_LAYER_WEIGHTS]
    n_in = 2 + N_LAYER_WEIGHTS + (N_CAST_IN if convert_next else 0)
    cast_in = refs[2 + N_LAYER_WEIGHTS:n_in]
    y_ref, meta_out_ref = refs[n_in:n_in + 2]
    cast_out = refs[n_in + 2:-4]
    state_ref, tail_ref, meta_state_ref, meta_tail_ref = refs[-4:]
    t = pl.program_id(1)

    @pl.when(pl.program_id(0) * pl.num_programs(1) + t == 0)
    def _():
        state_ref[...] = jnp.zeros(state_ref.shape, jnp.float32)
        tail_ref[...] = jnp.zeros(tail_ref.shape, jnp.float32)
        meta_out_ref[...] = _layer_body(meta_ref[...], weights, state_ref, tail_ref,
                                        gla_block=N_META, chunk=N_META)
        meta_state_ref[...] = state_ref[...]
        meta_tail_ref[...] = tail_ref[...]

    @pl.when(t == 0)
    def _():
        state_ref[...] = meta_state_ref[...]
        tail_ref[...] = meta_tail_ref[...]

    def cast_next():
        _cast_square_rows(cast_in, cast_out)
        _cast_w_in_blocks(cast_in, cast_out)

    y = _layer_body(x_ref[...], weights, state_ref, tail_ref, gla_block=gla_block, chunk=chunk,
                    cast_next=cast_next if convert_next else None)
    if apply_final_norm:
        y = y * lax.rsqrt(jnp.mean(y * y, axis=-1, keepdims=True) + EPS) * _vec(weights[0], "final_g")
    y_ref[...] = y


def _resident(shape):
    return pl.BlockSpec(shape, lambda b, t: (0,) * len(shape), pipeline_mode=pl.Buffered(1))


def _layer_slice(layer, shape):
    return pl.BlockSpec((None,) + shape, lambda b, t: (layer,) + (0,) * len(shape),
                        pipeline_mode=pl.Buffered(1))


N_CAST_IN = 5
_CAST_COLS = (N_GLA_COLS + LANES, N_CONV_COLS, D_MODEL, D_MODEL, D_MODEL)
F32_ROWS = 8
_CAST_BLOCKS_INLINE = (256, LANES)
_CAST_BLOCKS_ALONE = (1024, 640, 256)


def _cast_steps(conv_cols, gla_cols):
    assert N_CONV_COLS % conv_cols == 0 and (N_GLA_COLS + LANES) % gla_cols == 0
    return max(N_CONV_COLS // conv_cols, (N_GLA_COLS + LANES) // gla_cols)


def _cast_specs(layer, step_of, conv_cols, gla_cols, square_rows, square_block):
    assert square_rows % BF16_ROWS == 0 and (N_GLA_COLS + GATE_RANK) % F32_ROWS == 0
    n_conv = N_CONV_COLS // conv_cols
    n_gla = (N_GLA_COLS + LANES) // gla_cols

    def conv_block(*g):
        return jnp.minimum(step_of(*g), n_conv - 1)

    def gla_block(*g):
        return jnp.minimum(step_of(*g), n_gla - 1)

    def conv_window(*g):
        row = N_GLA_COLS + GATE_RANK + conv_cols * conv_block(*g)
        return (layer, pl.multiple_of(row, F32_ROWS), 0)

    in_specs = [
        pl.BlockSpec((pl.Element(1), pl.Element(conv_cols), pl.Element(D_MODEL)), conv_window),
        pl.BlockSpec((None, gla_cols, D_MODEL), lambda *g: (layer, gla_block(*g), 0)),
    ] + [pl.BlockSpec((None, square_rows, D_MODEL), lambda *g: (layer, square_block(*g), 0))] * 3
    out_specs = [
        pl.BlockSpec((D_MODEL, gla_cols), lambda *g: (0, gla_block(*g))),
        pl.BlockSpec((D_MODEL, conv_cols), lambda *g: (0, conv_block(*g))),
    ] + [pl.BlockSpec((square_rows, D_MODEL), lambda *g: (square_block(*g), 0))] * 3
    out_shape = [jax.ShapeDtypeStruct((D_MODEL, cols), jnp.bfloat16) for cols in _CAST_COLS]
    return in_specs, out_specs, out_shape


def _cast_layer(f32_mats, layer):
    conv_cols, gla_cols, square_rows = _CAST_BLOCKS_ALONE
    n_square = D_MODEL // square_rows
    in_specs, out_specs, out_shape = _cast_specs(
        layer, lambda i: i, conv_cols, gla_cols, square_rows,
        lambda i: jnp.minimum(i, n_square - 1))
    return pl.pallas_call(
        _cast_kernel,
        grid=(max(_cast_steps(conv_cols, gla_cols), n_square),),
        in_specs=in_specs,
        out_specs=out_specs,
        out_shape=out_shape,
        compiler_params=pltpu.CompilerParams(dimension_semantics=("arbitrary",),
                                             vmem_limit_bytes=VMEM_LIMIT_BYTES),
        name="cast_weights",
    )(*f32_mats)


def _layer_call(x, meta_h, small, big, layer, next_f32, *, tile, chunk, apply_final_norm):
    bsz, seq, _ = x.shape
    n_tiles = seq // tile
    gla_block = min(tile, GLA_BLOCK)
    assert seq % tile == 0 and tile % gla_block == 0 and gla_block % chunk == 0
    convert_next = next_f32 is not None
    kern = functools.partial(_layer_kernel, gla_block=gla_block, chunk=chunk,
                             apply_final_norm=apply_final_norm, convert_next=convert_next)
    tok_spec = pl.BlockSpec((None, tile, D_MODEL), lambda b, t: (b, t, 0))
    state_shape = (GLA_HEADS, GLA_DK, GLA_DV)
    tail_shape = (CONV_K - 1, D_MODEL)
    in_specs = [
        tok_spec,
        _resident((N_META, D_MODEL)),
        _layer_slice(layer, (N_VEC_ROWS, D_MODEL)),
        _layer_slice(layer, (GATE_RANK, GLA_K)),
        _resident((D_MODEL, N_GLA_COLS + LANES)),
        _resident((D_MODEL, N_CONV_COLS)),
        _resident((GLA_V, D_MODEL)),
        _resident((D_MODEL, D_MODEL)),
        _resident((D_MODEL, D_MODEL)),
    ]
    out_specs = [tok_spec, pl.BlockSpec((N_META, D_MODEL), lambda b, t: (0, 0))]
    out_shape = [jax.ShapeDtypeStruct(x.shape, jnp.float32),
                 jax.ShapeDtypeStruct((N_META, D_MODEL), jnp.float32)]
    args = [x, meta_h, *small, *big]
    if convert_next:
        steps = bsz * n_tiles
        assert D_MODEL % steps == 0
        assert steps >= _cast_steps(*_CAST_BLOCKS_INLINE)

        def step_of(b, t):
            return b * n_tiles + t

        cast = _cast_specs(layer + 1, step_of, *_CAST_BLOCKS_INLINE, D_MODEL // steps, step_of)
        in_specs.extend(cast[0])
        out_specs.extend(cast[1])
        out_shape.extend(cast[2])
        args.extend(next_f32)
    return pl.pallas_call(
        kern,
        grid=(bsz, n_tiles),
        in_specs=in_specs,
        out_specs=out_specs,
        out_shape=out_shape,
        scratch_shapes=[
            pltpu.VMEM(state_shape, jnp.float32),
            pltpu.VMEM(tail_shape, jnp.float32),
            pltpu.VMEM(state_shape, jnp.float32),
            pltpu.VMEM(tail_shape, jnp.float32),
        ],
        compiler_params=pltpu.CompilerParams(
            dimension_semantics=("arbitrary", "arbitrary"),
            vmem_limit_bytes=VMEM_LIMIT_BYTES),
        name="hybrid_layer",
    )(*args)


def kernel(x, meta, norm_g, w_in, w_gate_up, b_gate, gla_norm_g, w_o_gla, conv_w, w_o_conv, w_out,
           final_norm_g):
    depth = w_in.shape[0]
    rows = {
        "norm_g": norm_g[:, None, :],
        "gla_norm_g": gla_norm_g[:, None, :],
        "conv_w": conv_w,
        "final_g": jnp.broadcast_to(final_norm_g[None, None, :], (depth, 1, D_MODEL)),
        "b_gate": jnp.pad(b_gate, ((0, 0), (0, D_MODEL - GLA_K)))[:, None, :],
    }
    used = sum(count for _, count in _VEC_ROWS.values())
    vectors = jnp.concatenate([rows[name] for name in sorted(_VEC_ROWS, key=_VEC_ROWS.get)]
                              + [jnp.zeros((depth, N_VEC_ROWS - used, D_MODEL), jnp.float32)], axis=1)
    small = (vectors, _bf16(w_gate_up))
    w_in_t = jnp.swapaxes(w_in, 1, 2)
    next_f32 = (w_in_t, w_in_t, w_o_gla, w_o_conv, w_out)
    big = _cast_layer(next_f32, 0)
    meta_h = meta.astype(x.dtype)
    h = x
    for l in range(depth):
        last = l == depth - 1
        h, meta_h, *big = _layer_call(h, meta_h, small, big, l, None if last else next_f32,
                                      tile=TOKEN_TILE, chunk=CHUNK, apply_final_norm=last)
    return h
```

```python
import functools

import jax
import jax.numpy as jnp
from jax import lax
from jax.experimental import pallas as pl
from jax.experimental.pallas import tpu as pltpu

D_MODEL = 1024
N_META = 16
GLA_HEADS = 4
GLA_K = D_MODEL // 2
GLA_V = D_MODEL
GLA_DK = GLA_K // GLA_HEADS
GLA_DV = GLA_V // GLA_HEADS
GATE_RANK = 16
GATE_TAU = 16.0
CHUNK = 64
CONV_K = 3
EPS = 1e-6
LANES = 128
BF16_ROWS = 16

_GLA_SECTIONS = ("qk", "v", "r")
_CONV_SECTIONS = ("ch", "cb", "cc", "cz", "ga", "gb")
N_GLA_COLS = len(_GLA_SECTIONS) * D_MODEL
N_CONV_COLS = len(_CONV_SECTIONS) * D_MODEL

VMEM_LIMIT_BYTES = 60 * 1024 * 1024
TOKEN_TILE = 512
GLA_BLOCK = 256


def _sigmoid(z):
    return 0.5 * jnp.tanh(0.5 * z) + 0.5


def _silu(z):
    return z * _sigmoid(z)


def _log_sigmoid(z):
    return jnp.minimum(z, 0.0) - jnp.log(1.0 + jnp.exp(-jnp.abs(z)))


def _bf16(a):
    return a.astype(jnp.bfloat16)


def _dot(a, b):
    return jnp.dot(a, b, preferred_element_type=jnp.float32)


def _split2_bf16(a):
    hi = _bf16(a)
    lo = _bf16(a - hi.astype(jnp.float32))
    return hi, lo


def _gla_masks(block, chunk):
    ti = lax.broadcasted_iota(jnp.int32, (block, block), 0)
    si = lax.broadcasted_iota(jnp.int32, (block, block), 1)
    causal = ti >= si
    shift = chunk.bit_length() - 1
    same_chunk = lax.shift_right_logical(ti, shift) == lax.shift_right_logical(si, shift)
    return causal, _bf16((causal & same_chunk).astype(jnp.float32))


def _gla_prep(q, k, g, tril, chunk):
    block = q.shape[0]
    n_chunks = block // chunk
    g_hi, g_lo = _split2_bf16(g)
    b = _dot(tril, g_hi) + _dot(tril, g_lo)

    def rows(c):
        return slice(c * chunk, (c + 1) * chunk)

    b_last = [b[(c + 1) * chunk - 1:(c + 1) * chunk, :] for c in range(n_chunks)]
    base = [jnp.zeros((1, GLA_K), jnp.float32)]
    for c in range(n_chunks):
        base.append(base[c] + b_last[c])

    q_in = q * (jnp.exp(b) * (GLA_DK ** -0.5))
    k_in = k * jnp.exp(-b)
    q_in_c, q_dec_c, k_in_c, k_st_c, k_end_c = [], [], [], [], []
    for c in range(n_chunks):
        q_c = q_in[rows(c), :]
        q_in_c.append(_bf16(q_c))
        q_dec_c.append(_bf16(q_c * jnp.exp(base[c])))
        k_in_c.append(_bf16(k_in[rows(c), :]))
        k_st = k[rows(c), :] * jnp.exp(b_last[c] - b[rows(c), :])
        k_st_c.append(k_st)
        k_end_c.append(_bf16(k_st * jnp.exp(base[n_chunks] - base[c + 1])))
    q_dec = jnp.concatenate(q_dec_c, axis=0)
    k_end = jnp.concatenate(k_end_c, axis=0)
    block_decay = jnp.exp(base[n_chunks])

    zero_rows = jnp.zeros((chunk, GLA_K), jnp.bfloat16)
    k_seen = []
    for c in range(n_chunks):
        blocks = [_bf16(k_st_c[j] * jnp.exp(base[c] - base[j + 1])) for j in range(c)]
        blocks.append(k_in_c[c])
        blocks.extend([zero_rows] * (n_chunks - 1 - c))
        k_seen.append(jnp.concatenate(blocks, axis=0))
    return q_in_c, k_seen, q_dec, k_end, block_decay


def _gla_attend(operands, v, r, causal, gla_norm_g, state_ref):
    q_in_c, k_seen, q_dec, k_end, block_decay = operands
    n_chunks = len(q_in_c)
    nt_dims = (((1,), (1,)), ((), ()))
    o_heads = []
    for hd in range(GLA_HEADS):
        ks = slice(hd * GLA_DK, (hd + 1) * GLA_DK)
        vs = slice(hd * GLA_DV, (hd + 1) * GLA_DV)
        att = jnp.concatenate(
            [lax.dot_general(q_in_c[c][:, ks], k_seen[c][:, ks], nt_dims,
                             preferred_element_type=jnp.float32) for c in range(n_chunks)],
            axis=0)
        att = _bf16(jnp.where(causal, att, 0.0))
        st = state_ref[hd]
        oh = _dot(att, v[:, vs]) + _dot(q_dec[:, ks], _bf16(st))
        kv = lax.dot_general(k_end[:, ks], v[:, vs], (((0,), (0,)), ((), ())),
                             preferred_element_type=jnp.float32)
        decay_col = jnp.broadcast_to(block_decay[:, ks], (8, GLA_DK)).T[:, 0:1]
        state_ref[hd] = st * decay_col + kv
        oh = oh * lax.rsqrt(jnp.mean(oh * oh, axis=-1, keepdims=True) + EPS) * gla_norm_g[:, vs]
        o_heads.append(_bf16(oh * _silu(r[:, vs])))
    return jnp.concatenate(o_heads, axis=-1)


def _cast_w_in_blocks(f32_refs, bf16_refs):
    wt_conv_ref, wt_gla_ref = f32_refs[:2]
    o_gla_ref, o_conv_ref = bf16_refs[:2]
    o_conv_ref[...] = _bf16(wt_conv_ref[0].T)
    o_gla_ref[...] = _bf16(wt_gla_ref[...].T)


def _cast_square_rows(f32_refs, bf16_refs):
    for src_ref, dst_ref in zip(f32_refs[2:], bf16_refs[2:]):
        dst_ref[...] = _bf16(src_ref[...])


def _cast_kernel(*refs):
    _cast_w_in_blocks(refs[:N_CAST_IN], refs[N_CAST_IN:])
    _cast_square_rows(refs[:N_CAST_IN], refs[N_CAST_IN:])


_VEC_ROWS = {"norm_g": (0, 1), "gla_norm_g": (1, 1), "conv_w": (2, CONV_K), "final_g": (5, 1),
             "b_gate": (6, 1)}
N_VEC_ROWS = 8


def _vec(vec_ref, name):
    first, count = _VEC_ROWS[name]
    return vec_ref[first:first + count, :]


def _normed(x, vec_ref):
    return _bf16(x * lax.rsqrt(jnp.mean(x * x, axis=-1, keepdims=True) + EPS)
                 * _vec(vec_ref, "norm_g"))


def _exact_zero_after(a):
    bits = pltpu.bitcast(a, jnp.uint32)
    acc = bits[0:8, :]
    for i in range(1, bits.shape[0] // 8):
        acc = acc | bits[8 * i:8 * i + 8, :]
    zero = lax.shift_right_logical(lax.shift_right_logical(acc, jnp.uint32(16)), jnp.uint32(16))
    return zero.astype(jnp.float32)


def _layer_body(x, weights, state_ref, tail_ref, *, gla_block, chunk, cast_next=None, h=None,
                next_norm=None):
    vec_ref, w_gate_up_ref, w_gla_ref, w_conv_ref, w_o_gla_ref, w_o_conv_ref, w_out_ref = weights
    tile = x.shape[0]
    if h is None:
        h = _normed(x, vec_ref)

    def proj(name):
        if name in _GLA_SECTIONS:
            w_ref, i = w_gla_ref, _GLA_SECTIONS.index(name)
        else:
            w_ref, i = w_conv_ref, _CONV_SECTIONS.index(name)
        return _dot(h, w_ref[:, i * D_MODEL:(i + 1) * D_MODEL])

    r_glr = _dot(h, w_gla_ref[:, 2 * D_MODEL:])
    r = r_glr[:, :D_MODEL]
    glr = r_glr[:, D_MODEL:D_MODEL + GATE_RANK]
    qk = proj("qk")
    b_gate = _vec(vec_ref, "b_gate")[:, :GLA_K]
    if next_norm is not None:
        b_gate = b_gate + next_norm()[0:1, :GLA_K]
    z = _dot(_bf16(glr), w_gate_up_ref[...]) + b_gate
    g = _log_sigmoid(z) * (1.0 / GATE_TAU)
    blocks = [slice(s * gla_block, (s + 1) * gla_block) for s in range(tile // gla_block)]
    causal, tril = _gla_masks(gla_block, chunk)
    gla_norm_g = _vec(vec_ref, "gla_norm_g")

    def prep(blk):
        return _gla_prep(qk[blk, :GLA_K], qk[blk, GLA_K:], g[blk, :], tril, chunk)

    operands = prep(blocks[0])
    v = _bf16(proj("v"))
    gated = []
    for s, blk in enumerate(blocks):
        gated.append(_gla_attend(operands, v[blk, :], r[blk, :], causal, gla_norm_g, state_ref))
        if s + 1 < len(blocks):
            operands = prep(blocks[s + 1])
        if s == 0:
            cc = proj("cc")
            ch = proj("ch")
            cb = proj("cb")
    cz = proj("cz")
    if cast_next is not None:
        cast_next()
    y_gla = _dot(jnp.concatenate(gated, axis=0), w_o_gla_ref[...])

    merge_a = _sigmoid(proj("ga"))
    merge_b = _sigmoid(proj("gb"))

    u = cc * ch
    tail = tail_ref[...]
    tok = lax.broadcasted_iota(jnp.int32, (tile, D_MODEL), 0)
    u1 = jnp.where(tok == 0, tail[1:2, :], pltpu.roll(u, 1, 0))
    u2 = jnp.where(tok == 0, tail[0:1, :],
                   jnp.where(tok == 1, tail[1:2, :], pltpu.roll(u, 2, 0)))
    tail_ref[...] = u[tile - 2:tile, :]
    cw = _vec(vec_ref, "conv_w")
    y_c = cw[0:1, :] * u2 + cw[1:2, :] * u1 + cw[2:3, :] * u
    y_c = cb * y_c * _silu(cz)
    y_conv = _dot(_bf16(y_c), w_o_conv_ref[...])

    merged = merge_a * y_gla + merge_b * y_conv
    return x + _dot(_bf16(merged), w_out_ref[...])


N_LAYER_WEIGHTS = 7


def _layer_kernel(*refs, gla_block, chunk, apply_final_norm, convert_next):
    x_ref, meta_ref = refs[:2]
    weights = refs[2:2 + N_LAYER_WEIGHTS]
    n_in = 2 + N_LAYER_WEIGHTS + (N_CAST_IN if convert_next else 0) + 1
    cast_in = refs[2 + N_LAYER_WEIGHTS:n_in - 1]
    x_next_ref = refs[n_in - 1]
    y_ref, meta_out_ref = refs[n_in:n_in + 2]
    cast_out = refs[n_in + 2:-6]
    state_ref, tail_ref, meta_state_ref, meta_tail_ref, h_ref, h_next_ref = refs[-6:]
    t = pl.program_id(1)

    @pl.when(pl.program_id(0) * pl.num_programs(1) + t == 0)
    def _():
        state_ref[...] = jnp.zeros(state_ref.shape, jnp.float32)
        tail_ref[...] = jnp.zeros(tail_ref.shape, jnp.float32)
        meta_out_ref[...] = _layer_body(meta_ref[...], weights, state_ref, tail_ref,
                                        gla_block=N_META, chunk=N_META)
        meta_state_ref[...] = state_ref[...]
        meta_tail_ref[...] = tail_ref[...]

    @pl.when(t == 0)
    def _():
        state_ref[...] = meta_state_ref[...]
        tail_ref[...] = meta_tail_ref[...]

    def cast_next():
        _cast_square_rows(cast_in, cast_out)
        _cast_w_in_blocks(cast_in, cast_out)

    @pl.when(pl.program_id(0) * pl.num_programs(1) + t == 0)
    def _():
        h_next_ref[...] = _normed(x_ref[...], weights[0])

    h_ref[...] = h_next_ref[...]

    def next_norm():
        h_next = _normed(x_next_ref[...], weights[0])
        h_next_ref[...] = h_next
        return _exact_zero_after(h_next)

    y = _layer_body(x_ref[...], weights, state_ref, tail_ref, gla_block=gla_block, chunk=chunk,
                    cast_next=cast_next if convert_next else None, h=h_ref[...],
                    next_norm=next_norm)
    if apply_final_norm:
        y = y * lax.rsqrt(jnp.mean(y * y, axis=-1, keepdims=True) + EPS) * _vec(weights[0], "final_g")
    y_ref[...] = y


def _resident(shape):
    return pl.BlockSpec(shape, lambda b, t: (0,) * len(shape), pipeline_mode=pl.Buffered(1))


def _layer_slice(layer, shape):
    return pl.BlockSpec((None,) + shape, lambda b, t: (layer,) + (0,) * len(shape),
                        pipeline_mode=pl.Buffered(1))


N_CAST_IN = 5
_CAST_COLS = (N_GLA_COLS + LANES, N_CONV_COLS, D_MODEL, D_MODEL, D_MODEL)
F32_ROWS = 8
_CAST_BLOCKS_INLINE = (256, LANES)
_CAST_BLOCKS_ALONE = (1024, 640, 256)


def _cast_steps(conv_cols, gla_cols):
    assert N_CONV_COLS % conv_cols == 0 and (N_GLA_COLS + LANES) % gla_cols == 0
    return max(N_CONV_COLS // conv_cols, (N_GLA_COLS + LANES) // gla_cols)


def _cast_specs(layer, step_of, conv_cols, gla_cols, square_rows, square_block):
    assert square_rows % BF16_ROWS == 0 and (N_GLA_COLS + GATE_RANK) % F32_ROWS == 0
    n_conv = N_CONV_COLS // conv_cols
    n_gla = (N_GLA_COLS + LANES) // gla_cols

    def conv_block(*g):
        return jnp.minimum(step_of(*g), n_conv - 1)

    def gla_block(*g):
        return jnp.minimum(step_of(*g), n_gla - 1)

    def conv_window(*g):
        row = N_GLA_COLS + GATE_RANK + conv_cols * conv_block(*g)
        return (layer, pl.multiple_of(row, F32_ROWS), 0)

    in_specs = [
        pl.BlockSpec((pl.Element(1), pl.Element(conv_cols), pl.Element(D_MODEL)), conv_window),
        pl.BlockSpec((None, gla_cols, D_MODEL), lambda *g: (layer, gla_block(*g), 0)),
    ] + [pl.BlockSpec((None, square_rows, D_MODEL), lambda *g: (layer, square_block(*g), 0))] * 3
    out_specs = [
        pl.BlockSpec((D_MODEL, gla_cols), lambda *g: (0, gla_block(*g))),
        pl.BlockSpec((D_MODEL, conv_cols), lambda *g: (0, conv_block(*g))),
    ] + [pl.BlockSpec((square_rows, D_MODEL), lambda *g: (square_block(*g), 0))] * 3
    out_shape = [jax.ShapeDtypeStruct((D_MODEL, cols), jnp.bfloat16) for cols in _CAST_COLS]
    return in_specs, out_specs, out_shape


def _cast_layer(f32_mats, layer):
    conv_cols, gla_cols, square_rows = _CAST_BLOCKS_ALONE
    n_square = D_MODEL // square_rows
    in_specs, out_specs, out_shape = _cast_specs(
        layer, lambda i: i, conv_cols, gla_cols, square_rows,
        lambda i: jnp.minimum(i, n_square - 1))
    return pl.pallas_call(
        _cast_kernel,
        grid=(max(_cast_steps(conv_cols, gla_cols), n_square),),
        in_specs=in_specs,
        out_specs=out_specs,
        out_shape=out_shape,
        compiler_params=pltpu.CompilerParams(dimension_semantics=("arbitrary",),
                                             vmem_limit_bytes=VMEM_LIMIT_BYTES),
        name="cast_weights",
    )(*f32_mats)


def _layer_call(x, meta_h, small, big, layer, next_f32, *, tile, chunk, apply_final_norm):
    bsz, seq, _ = x.shape
    n_tiles = seq // tile
    gla_block = min(tile, GLA_BLOCK)
    assert seq % tile == 0 and tile % gla_block == 0 and gla_block % chunk == 0
    convert_next = next_f32 is not None
    kern = functools.partial(_layer_kernel, gla_block=gla_block, chunk=chunk,
                             apply_final_norm=apply_final_norm, convert_next=convert_next)
    tok_spec = pl.BlockSpec((None, tile, D_MODEL), lambda b, t: (b, t, 0))
    state_shape = (GLA_HEADS, GLA_DK, GLA_DV)
    tail_shape = (CONV_K - 1, D_MODEL)
    in_specs = [
        tok_spec,
        _resident((N_META, D_MODEL)),
        _layer_slice(layer, (N_VEC_ROWS, D_MODEL)),
        _layer_slice(layer, (GATE_RANK, GLA_K)),
        _resident((D_MODEL, N_GLA_COLS + LANES)),
        _resident((D_MODEL, N_CONV_COLS)),
        _resident((GLA_V, D_MODEL)),
        _resident((D_MODEL, D_MODEL)),
        _resident((D_MODEL, D_MODEL)),
    ]
    out_specs = [tok_spec, pl.BlockSpec((N_META, D_MODEL), lambda b, t: (0, 0))]
    out_shape = [jax.ShapeDtypeStruct(x.shape, jnp.float32),
                 jax.ShapeDtypeStruct((N_META, D_MODEL), jnp.float32)]
    args = [x, meta_h, *small, *big]

    def next_tile(b, t):
        nxt = jnp.minimum(b * n_tiles + t + 1, bsz * n_tiles - 1)
        return (nxt // n_tiles, nxt % n_tiles, 0)
    if convert_next:
        steps = bsz * n_tiles
        assert D_MODEL % steps == 0
        assert steps >= _cast_steps(*_CAST_BLOCKS_INLINE)

        def step_of(b, t):
            return b * n_tiles + t

        cast = _cast_specs(layer + 1, step_of, *_CAST_BLOCKS_INLINE, D_MODEL // steps, step_of)
        in_specs.extend(cast[0])
        out_specs.extend(cast[1])
        out_shape.extend(cast[2])
        args.extend(next_f32)
    in_specs.append(pl.BlockSpec((None, tile, D_MODEL), next_tile))
    args.append(x)
    return pl.pallas_call(
        kern,
        grid=(bsz, n_tiles),
        in_specs=in_specs,
        out_specs=out_specs,
        out_shape=out_shape,
        scratch_shapes=[
            pltpu.VMEM(state_shape, jnp.float32),
            pltpu.VMEM(tail_shape, jnp.float32),
            pltpu.VMEM(state_shape, jnp.float32),
            pltpu.VMEM(tail_shape, jnp.float32),
            pltpu.VMEM((tile, D_MODEL), jnp.bfloat16),
            pltpu.VMEM((tile, D_MODEL), jnp.bfloat16),
        ],
        compiler_params=pltpu.CompilerParams(
            dimension_semantics=("arbitrary", "arbitrary"),
            vmem_limit_bytes=VMEM_LIMIT_BYTES),
        name="hybrid_layer",
    )(*args)


def kernel(x, meta, norm_g, w_in, w_gate_up, b_gate, gla_norm_g, w_o_gla, conv_w, w_o_conv, w_out,
           final_norm_g):
    depth = w_in.shape[0]
    rows = {
        "norm_g": norm_g[:, None, :],
        "gla_norm_g": gla_norm_g[:, None, :],
        "conv_w": conv_w,
        "final_g": jnp.broadcast_to(final_norm_g[None, None, :], (depth, 1, D_MODEL)),
        "b_gate": jnp.pad(b_gate, ((0, 0), (0, D_MODEL - GLA_K)))[:, None, :],
    }
    used = sum(count for _, count in _VEC_ROWS.values())
    vectors = jnp.concatenate([rows[name] for name in sorted(_VEC_ROWS, key=_VEC_ROWS.get)]
                              + [jnp.zeros((depth, N_VEC_ROWS - used, D_MODEL), jnp.float32)], axis=1)
    small = (vectors, _bf16(w_gate_up))
    w_in_t = jnp.swapaxes(w_in, 1, 2)
    next_f32 = (w_in_t, w_in_t, w_o_gla, w_o_conv, w_out)
    big = _cast_layer(next_f32, 0)
    meta_h = meta.astype(x.dtype)
    h = x
    for l in range(depth):
        last = l == depth - 1
        h, meta_h, *big = _layer_call(h, meta_h, small, big, l, None if last else next_f32,
                                      tile=TOKEN_TILE, chunk=CHUNK, apply_final_norm=last)
    return h
```

```python
import functools

import jax
import jax.numpy as jnp
from jax import lax
from jax.experimental import pallas as pl
from jax.experimental.pallas import tpu as pltpu

D_MODEL = 1024
N_META = 16
GLA_HEADS = 4
GLA_K = D_MODEL // 2
GLA_V = D_MODEL
GLA_DK = GLA_K // GLA_HEADS
GLA_DV = GLA_V // GLA_HEADS
GATE_RANK = 16
GATE_TAU = 16.0
CHUNK = 64
CONV_K = 3
EPS = 1e-6
LANES = 128
BF16_ROWS = 16

_GLA_SECTIONS = ("qk", "v", "r")
_CONV_SECTIONS = ("ch", "cb", "cc", "cz", "ga", "gb")
N_GLA_COLS = len(_GLA_SECTIONS) * D_MODEL
N_CONV_COLS = len(_CONV_SECTIONS) * D_MODEL

VMEM_LIMIT_BYTES = 60 * 1024 * 1024
TOKEN_TILE = 512
GLA_BLOCK = 256


def _sigmoid(z):
    return 0.5 * jnp.tanh(0.5 * z) + 0.5


def _silu(z):
    return z * _sigmoid(z)


def _log_sigmoid(z):
    return jnp.minimum(z, 0.0) - jnp.log(1.0 + jnp.exp(-jnp.abs(z)))


def _bf16(a):
    return a.astype(jnp.bfloat16)


def _dot(a, b):
    return jnp.dot(a, b, preferred_element_type=jnp.float32)


def _split2_bf16(a):
    hi = _bf16(a)
    lo = _bf16(a - hi.astype(jnp.float32))
    return hi, lo


def _gla_masks(block, chunk):
    ti = lax.broadcasted_iota(jnp.int32, (block, block), 0)
    si = lax.broadcasted_iota(jnp.int32, (block, block), 1)
    causal = ti >= si
    shift = chunk.bit_length() - 1
    same_chunk = lax.shift_right_logical(ti, shift) == lax.shift_right_logical(si, shift)
    return causal, _bf16((causal & same_chunk).astype(jnp.float32))


def _gla_prep(q, k, g, tril, chunk):
    block = q.shape[0]
    n_chunks = block // chunk
    g_hi, g_lo = _split2_bf16(g)
    b = _dot(tril, g_hi) + _dot(tril, g_lo)

    def rows(c):
        return slice(c * chunk, (c + 1) * chunk)

    b_last = [b[(c + 1) * chunk - 1:(c + 1) * chunk, :] for c in range(n_chunks)]
    base = [jnp.zeros((1, GLA_K), jnp.float32)]
    for c in range(n_chunks):
        base.append(base[c] + b_last[c])

    q_in = q * (jnp.exp(b) * (GLA_DK ** -0.5))
    k_in = k * jnp.exp(-b)
    q_in_c, q_dec_c, k_in_c, k_st_c, k_end_c = [], [], [], [], []
    for c in range(n_chunks):
        q_c = q_in[rows(c), :]
        q_in_c.append(_bf16(q_c))
        q_dec_c.append(_bf16(q_c * jnp.exp(base[c])))
        k_in_c.append(_bf16(k_in[rows(c), :]))
        k_st = k[rows(c), :] * jnp.exp(b_last[c] - b[rows(c), :])
        k_st_c.append(k_st)
        k_end_c.append(_bf16(k_st * jnp.exp(base[n_chunks] - base[c + 1])))
    q_dec = jnp.concatenate(q_dec_c, axis=0)
    k_end = jnp.concatenate(k_end_c, axis=0)
    block_decay = jnp.exp(base[n_chunks])

    zero_rows = jnp.zeros((chunk, GLA_K), jnp.bfloat16)
    k_seen = []
    for c in range(n_chunks):
        blocks = [_bf16(k_st_c[j] * jnp.exp(base[c] - base[j + 1])) for j in range(c)]
        blocks.append(k_in_c[c])
        blocks.extend([zero_rows] * (n_chunks - 1 - c))
        k_seen.append(jnp.concatenate(blocks, axis=0))
    return q_in_c, k_seen, q_dec, k_end, block_decay


def _gla_attend(operands, v, r, causal, gla_norm_g, state_ref):
    q_in_c, k_seen, q_dec, k_end, block_decay = operands
    n_chunks = len(q_in_c)
    nt_dims = (((1,), (1,)), ((), ()))
    o_heads = []
    for hd in range(GLA_HEADS):
        ks = slice(hd * GLA_DK, (hd + 1) * GLA_DK)
        vs = slice(hd * GLA_DV, (hd + 1) * GLA_DV)
        att = jnp.concatenate(
            [lax.dot_general(q_in_c[c][:, ks], k_seen[c][:, ks], nt_dims,
                             preferred_element_type=jnp.float32) for c in range(n_chunks)],
            axis=0)
        att = _bf16(jnp.where(causal, att, 0.0))
        st = state_ref[hd]
        oh = _dot(att, v[:, vs]) + _dot(q_dec[:, ks], _bf16(st))
        kv = lax.dot_general(k_end[:, ks], v[:, vs], (((0,), (0,)), ((), ())),
                             preferred_element_type=jnp.float32)
        decay_col = jnp.broadcast_to(block_decay[:, ks], (8, GLA_DK)).T[:, 0:1]
        state_ref[hd] = st * decay_col + kv
        oh = oh * lax.rsqrt(jnp.mean(oh * oh, axis=-1, keepdims=True) + EPS) * gla_norm_g[:, vs]
        o_heads.append(_bf16(oh * _silu(r[:, vs])))
    return jnp.concatenate(o_heads, axis=-1)


def _cast_w_in_blocks(f32_refs, bf16_refs):
    wt_conv_ref, wt_gla_ref = f32_refs[:2]
    o_gla_ref, o_conv_ref = bf16_refs[:2]
    o_conv_ref[...] = _bf16(wt_conv_ref[0].T)
    o_gla_ref[...] = _bf16(wt_gla_ref[...].T)


def _cast_square_rows(f32_refs, bf16_refs):
    for src_ref, dst_ref in zip(f32_refs[2:], bf16_refs[2:]):
        dst_ref[...] = _bf16(src_ref[...])


def _cast_kernel(*refs):
    _cast_w_in_blocks(refs[:N_CAST_IN], refs[N_CAST_IN:])
    _cast_square_rows(refs[:N_CAST_IN], refs[N_CAST_IN:])


def _layer_body(x, weights, state_ref, tail_ref, *, gla_block, chunk, cast_next=None):
    (norm_g_ref, w_gate_up_ref, b_gate_ref, gla_norm_g_ref, conv_w_ref,
     w_gla_ref, w_conv_ref, w_o_gla_ref, w_o_conv_ref, w_out_ref) = weights
    tile = x.shape[0]
    h = x * lax.rsqrt(jnp.mean(x * x, axis=-1, keepdims=True) + EPS) * norm_g_ref[...]
    h = _bf16(h)

    def proj(name):
        if name in _GLA_SECTIONS:
            w_ref, i = w_gla_ref, _GLA_SECTIONS.index(name)
        else:
            w_ref, i = w_conv_ref, _CONV_SECTIONS.index(name)
        return _dot(h, w_ref[:, i * D_MODEL:(i + 1) * D_MODEL])

    r_glr = _dot(h, w_gla_ref[:, 2 * D_MODEL:])
    r = r_glr[:, :D_MODEL]
    glr = r_glr[:, D_MODEL:D_MODEL + GATE_RANK]
    qk = proj("qk")
    z = _dot(_bf16(glr), w_gate_up_ref[...]) + b_gate_ref[...]
    g = _log_sigmoid(z) * (1.0 / GATE_TAU)
    blocks = [slice(s * gla_block, (s + 1) * gla_block) for s in range(tile // gla_block)]
    causal, tril = _gla_masks(gla_block, chunk)
    gla_norm_g = gla_norm_g_ref[...]

    def prep(blk):
        return _gla_prep(qk[blk, :GLA_K], qk[blk, GLA_K:], g[blk, :], tril, chunk)

    operands = prep(blocks[0])
    v = _bf16(proj("v"))
    gated = []
    for s, blk in enumerate(blocks):
        gated.append(_gla_attend(operands, v[blk, :], r[blk, :], causal, gla_norm_g, state_ref))
        if s + 1 < len(blocks):
            operands = prep(blocks[s + 1])
        if s == 0:
            cc = proj("cc")
            ch = proj("ch")
            cb = proj("cb")
    cz = proj("cz")
    if cast_next is not None:
        cast_next()
    y_gla = _dot(jnp.concatenate(gated, axis=0), w_o_gla_ref[...])

    merge_a = _sigmoid(proj("ga"))
    merge_b = _sigmoid(proj("gb"))

    u = cc * ch
    tail = tail_ref[...]
    tok = lax.broadcasted_iota(jnp.int32, (tile, D_MODEL), 0)
    u1 = jnp.where(tok == 0, tail[1:2, :], pltpu.roll(u, 1, 0))
    u2 = jnp.where(tok == 0, tail[0:1, :],
                   jnp.where(tok == 1, tail[1:2, :], pltpu.roll(u, 2, 0)))
    tail_ref[...] = u[tile - 2:tile, :]
    cw = conv_w_ref[...]
    y_c = cw[0:1, :] * u2 + cw[1:2, :] * u1 + cw[2:3, :] * u
    y_c = cb * y_c * _silu(cz)
    y_conv = _dot(_bf16(y_c), w_o_conv_ref[...])

    merged = merge_a * y_gla + merge_b * y_conv
    return x + _dot(_bf16(merged), w_out_ref[...])


N_LAYER_WEIGHTS = 10


def _layer_kernel(*refs, n_tiles, gla_block, chunk, apply_final_norm, convert_next):
    x_ref, meta_ref, final_g_ref = refs[:3]
    weights = refs[3:3 + N_LAYER_WEIGHTS]
    n_in = 3 + N_LAYER_WEIGHTS + (N_CAST_IN if convert_next else 0)
    cast_in = refs[3 + N_LAYER_WEIGHTS:n_in]
    y_ref, meta_out_ref = refs[n_in:n_in + 2]
    cast_out = refs[n_in + 2:-5]
    state_ref, tail_ref, meta_state_ref, meta_tail_ref, step_ref = refs[-5:]
    step = step_ref[0]
    step_ref[0] = step + 1
    t = lax.rem(step, n_tiles)

    @pl.when(step == 0)
    def _():
        state_ref[...] = jnp.zeros(state_ref.shape, jnp.float32)
        tail_ref[...] = jnp.zeros(tail_ref.shape, jnp.float32)
        meta_out_ref[...] = _layer_body(meta_ref[...], weights, state_ref, tail_ref,
                                        gla_block=N_META, chunk=N_META)
        meta_state_ref[...] = state_ref[...]
        meta_tail_ref[...] = tail_ref[...]

    @pl.when(t == 0)
    def _():
        state_ref[...] = meta_state_ref[...]
        tail_ref[...] = meta_tail_ref[...]

    def cast_next():
        _cast_square_rows(cast_in, cast_out)
        _cast_w_in_blocks(cast_in, cast_out)

    y = _layer_body(x_ref[...], weights, state_ref, tail_ref, gla_block=gla_block, chunk=chunk,
                    cast_next=cast_next if convert_next else None)
    if apply_final_norm:
        y = y * lax.rsqrt(jnp.mean(y * y, axis=-1, keepdims=True) + EPS) * final_g_ref[...]
    y_ref[...] = y


def _pipeline_kernel(*refs, n_io, grid, in_specs, out_specs, inner):
    io_refs, scratch = refs[:n_io], refs[n_io:]
    scratch[-1][0] = 0

    def body(*block_refs):
        inner(*block_refs, *scratch)

    pltpu.emit_pipeline(body, grid=grid, in_specs=in_specs, out_specs=out_specs)(*io_refs)


def _resident(shape):
    return pl.BlockSpec(shape, lambda b, t: (0,) * len(shape), pipeline_mode=pl.Buffered(1))


def _layer_slice(layer, shape):
    return pl.BlockSpec((None,) + shape, lambda b, t: (layer,) + (0,) * len(shape),
                        pipeline_mode=pl.Buffered(1))


N_CAST_IN = 5
_CAST_COLS = (N_GLA_COLS + LANES, N_CONV_COLS, D_MODEL, D_MODEL, D_MODEL)
F32_ROWS = 8
_CAST_BLOCKS_INLINE = (256, LANES)
_CAST_BLOCKS_ALONE = (1024, 640, 256)


def _cast_steps(conv_cols, gla_cols):
    assert N_CONV_COLS % conv_cols == 0 and (N_GLA_COLS + LANES) % gla_cols == 0
    return max(N_CONV_COLS // conv_cols, (N_GLA_COLS + LANES) // gla_cols)


def _cast_specs(layer, step_of, conv_cols, gla_cols, square_rows, square_block):
    assert square_rows % BF16_ROWS == 0 and (N_GLA_COLS + GATE_RANK) % F32_ROWS == 0
    n_conv = N_CONV_COLS // conv_cols
    n_gla = (N_GLA_COLS + LANES) // gla_cols

    def conv_block(*g):
        return jnp.minimum(step_of(*g), n_conv - 1)

    def gla_block(*g):
        return jnp.minimum(step_of(*g), n_gla - 1)

    def conv_window(*g):
        row = N_GLA_COLS + GATE_RANK + conv_cols * conv_block(*g)
        return (layer, pl.multiple_of(row, F32_ROWS), 0)

    in_specs = [
        pl.BlockSpec((pl.Element(1), pl.Element(conv_cols), pl.Element(D_MODEL)), conv_window),
        pl.BlockSpec((None, gla_cols, D_MODEL), lambda *g: (layer, gla_block(*g), 0)),
    ] + [pl.BlockSpec((None, square_rows, D_MODEL), lambda *g: (layer, square_block(*g), 0))] * 3
    out_specs = [
        pl.BlockSpec((D_MODEL, gla_cols), lambda *g: (0, gla_block(*g))),
        pl.BlockSpec((D_MODEL, conv_cols), lambda *g: (0, conv_block(*g))),
    ] + [pl.BlockSpec((square_rows, D_MODEL), lambda *g: (square_block(*g), 0))] * 3
    out_shape = [jax.ShapeDtypeStruct((D_MODEL, cols), jnp.bfloat16) for cols in _CAST_COLS]
    return in_specs, out_specs, out_shape


def _cast_layer(f32_mats, layer):
    conv_cols, gla_cols, square_rows = _CAST_BLOCKS_ALONE
    n_square = D_MODEL // square_rows
    in_specs, out_specs, out_shape = _cast_specs(
        layer, lambda i: i, conv_cols, gla_cols, square_rows,
        lambda i: jnp.minimum(i, n_square - 1))
    return pl.pallas_call(
        _cast_kernel,
        grid=(max(_cast_steps(conv_cols, gla_cols), n_square),),
        in_specs=in_specs,
        out_specs=out_specs,
        out_shape=out_shape,
        compiler_params=pltpu.CompilerParams(dimension_semantics=("arbitrary",),
                                             vmem_limit_bytes=VMEM_LIMIT_BYTES),
        name="cast_weights",
    )(*f32_mats)


def _layer_call(x, meta_h, small, big, final_g, layer, next_f32, *, tile, chunk,
                apply_final_norm):
    bsz, seq, _ = x.shape
    n_tiles = seq // tile
    gla_block = min(tile, GLA_BLOCK)
    assert seq % tile == 0 and tile % gla_block == 0 and gla_block % chunk == 0
    convert_next = next_f32 is not None
    inner = functools.partial(_layer_kernel, n_tiles=n_tiles, gla_block=gla_block, chunk=chunk,
                              apply_final_norm=apply_final_norm, convert_next=convert_next)
    tok_spec = pl.BlockSpec((None, tile, D_MODEL), lambda b, t: (b, t, 0))
    state_shape = (GLA_HEADS, GLA_DK, GLA_DV)
    tail_shape = (CONV_K - 1, D_MODEL)
    in_specs = [
        tok_spec,
        _resident((N_META, D_MODEL)),
        _resident((1, D_MODEL)),
        _layer_slice(layer, (1, D_MODEL)),
        _layer_slice(layer, (GATE_RANK, GLA_K)),
        _layer_slice(layer, (1, GLA_K)),
        _layer_slice(layer, (1, GLA_V)),
        _layer_slice(layer, (CONV_K, D_MODEL)),
        _resident((D_MODEL, N_GLA_COLS + LANES)),
        _resident((D_MODEL, N_CONV_COLS)),
        _resident((GLA_V, D_MODEL)),
        _resident((D_MODEL, D_MODEL)),
        _resident((D_MODEL, D_MODEL)),
    ]
    out_specs = [tok_spec, pl.BlockSpec((N_META, D_MODEL), lambda b, t: (0, 0))]
    out_shape = [jax.ShapeDtypeStruct(x.shape, jnp.float32),
                 jax.ShapeDtypeStruct((N_META, D_MODEL), jnp.float32)]
    args = [x, meta_h, final_g, *small, *big]
    if convert_next:
        steps = bsz * n_tiles
        assert D_MODEL % steps == 0
        assert steps >= _cast_steps(*_CAST_BLOCKS_INLINE)

        def step_of(b, t):
            return b * n_tiles + t

        cast = _cast_specs(layer + 1, step_of, *_CAST_BLOCKS_INLINE, D_MODEL // steps, step_of)
        in_specs.extend(cast[0])
        out_specs.extend(cast[1])
        out_shape.extend(cast[2])
        args.extend(next_f32)
    kern = functools.partial(_pipeline_kernel, n_io=len(args) + len(out_shape),
                             grid=(bsz, n_tiles), in_specs=in_specs, out_specs=out_specs,
                             inner=inner)
    return pl.pallas_call(
        kern,
        in_specs=[pl.BlockSpec(memory_space=pl.ANY)] * len(args),
        out_specs=[pl.BlockSpec(memory_space=pl.ANY)] * len(out_shape),
        out_shape=out_shape,
        scratch_shapes=[
            pltpu.VMEM(state_shape, jnp.float32),
            pltpu.VMEM(tail_shape, jnp.float32),
            pltpu.VMEM(state_shape, jnp.float32),
            pltpu.VMEM(tail_shape, jnp.float32),
            pltpu.SMEM((1,), jnp.int32),
        ],
        compiler_params=pltpu.CompilerParams(vmem_limit_bytes=VMEM_LIMIT_BYTES),
        name="hybrid_layer",
    )(*args)


def kernel(x, meta, norm_g, w_in, w_gate_up, b_gate, gla_norm_g, w_o_gla, conv_w, w_o_conv, w_out,
           final_norm_g):
    depth = w_in.shape[0]
    small = (norm_g.reshape(depth, 1, D_MODEL), _bf16(w_gate_up), b_gate.reshape(depth, 1, GLA_K),
             gla_norm_g.reshape(depth, 1, GLA_V), conv_w)
    w_in_t = jnp.swapaxes(w_in, 1, 2)
    next_f32 = (w_in_t, w_in_t, w_o_gla, w_o_conv, w_out)
    final_g = final_norm_g.reshape(1, D_MODEL)
    meta_h = meta.astype(x.dtype)
    h = x
    for l in range(depth):
        last = l == depth - 1
        big = _cast_layer(next_f32, l)
        h, meta_h = _layer_call(h, meta_h, small, big, final_g, l, None,
                                tile=TOKEN_TILE, chunk=CHUNK, apply_final_norm=last)
    return h
```

```python
import functools

import jax
import jax.numpy as jnp
from jax import lax
from jax.experimental import pallas as pl
from jax.experimental.pallas import tpu as pltpu

D_MODEL = 1024
N_META = 16
GLA_HEADS = 4
GLA_K = D_MODEL // 2
GLA_V = D_MODEL
GLA_DK = GLA_K // GLA_HEADS
GLA_DV = GLA_V // GLA_HEADS
GATE_RANK = 16
GATE_TAU = 16.0
CHUNK = 64
CONV_K = 3
EPS = 1e-6
LANES = 128
BF16_ROWS = 16

_GLA_SECTIONS = ("qk", "v", "r")
_CONV_SECTIONS = ("ch", "cb", "cc", "cz", "ga", "gb")
N_GLA_COLS = len(_GLA_SECTIONS) * D_MODEL
N_CONV_COLS = len(_CONV_SECTIONS) * D_MODEL

VMEM_LIMIT_BYTES = 60 * 1024 * 1024
TOKEN_TILE = 512
GLA_BLOCK = 256
X_RING_SLOTS = 3


def _sigmoid(z):
    return 0.5 * jnp.tanh(0.5 * z) + 0.5


def _silu(z):
    return z * _sigmoid(z)


def _log_sigmoid(z):
    return jnp.minimum(z, 0.0) - jnp.log(1.0 + jnp.exp(-jnp.abs(z)))


def _bf16(a):
    return a.astype(jnp.bfloat16)


def _dot(a, b):
    return jnp.dot(a, b, preferred_element_type=jnp.float32)


def _split2_bf16(a):
    hi = _bf16(a)
    lo = _bf16(a - hi.astype(jnp.float32))
    return hi, lo


def _gla_masks(block, chunk):
    ti = lax.broadcasted_iota(jnp.int32, (block, block), 0)
    si = lax.broadcasted_iota(jnp.int32, (block, block), 1)
    causal = ti >= si
    shift = chunk.bit_length() - 1
    same_chunk = lax.shift_right_logical(ti, shift) == lax.shift_right_logical(si, shift)
    return causal, _bf16((causal & same_chunk).astype(jnp.float32))


def _gla_prep(q, k, g, tril, chunk):
    block = q.shape[0]
    n_chunks = block // chunk
    g_hi, g_lo = _split2_bf16(g)
    b = _dot(tril, g_hi) + _dot(tril, g_lo)

    def rows(c):
        return slice(c * chunk, (c + 1) * chunk)

    b_last = [b[(c + 1) * chunk - 1:(c + 1) * chunk, :] for c in range(n_chunks)]
    base = [jnp.zeros((1, GLA_K), jnp.float32)]
    for c in range(n_chunks):
        base.append(base[c] + b_last[c])

    q_in = q * (jnp.exp(b) * (GLA_DK ** -0.5))
    k_in = k * jnp.exp(-b)
    q_in_c, q_dec_c, k_in_c, k_st_c, k_end_c = [], [], [], [], []
    for c in range(n_chunks):
        q_c = q_in[rows(c), :]
        q_in_c.append(_bf16(q_c))
        q_dec_c.append(_bf16(q_c * jnp.exp(base[c])))
        k_in_c.append(_bf16(k_in[rows(c), :]))
        k_st = k[rows(c), :] * jnp.exp(b_last[c] - b[rows(c), :])
        k_st_c.append(k_st)
        k_end_c.append(_bf16(k_st * jnp.exp(base[n_chunks] - base[c + 1])))
    q_dec = jnp.concatenate(q_dec_c, axis=0)
    k_end = jnp.concatenate(k_end_c, axis=0)
    block_decay = jnp.exp(base[n_chunks])

    zero_rows = jnp.zeros((chunk, GLA_K), jnp.bfloat16)
    k_seen = []
    for c in range(n_chunks):
        blocks = [_bf16(k_st_c[j] * jnp.exp(base[c] - base[j + 1])) for j in range(c)]
        blocks.append(k_in_c[c])
        blocks.extend([zero_rows] * (n_chunks - 1 - c))
        k_seen.append(jnp.concatenate(blocks, axis=0))
    return q_in_c, k_seen, q_dec, k_end, block_decay


def _gla_attend(operands, v, r, causal, gla_norm_g, state_ref):
    q_in_c, k_seen, q_dec, k_end, block_decay = operands
    n_chunks = len(q_in_c)
    nt_dims = (((1,), (1,)), ((), ()))
    o_heads = []
    for hd in range(GLA_HEADS):
        ks = slice(hd * GLA_DK, (hd + 1) * GLA_DK)
        vs = slice(hd * GLA_DV, (hd + 1) * GLA_DV)
        att = jnp.concatenate(
            [lax.dot_general(q_in_c[c][:, ks], k_seen[c][:, ks], nt_dims,
                             preferred_element_type=jnp.float32) for c in range(n_chunks)],
            axis=0)
        att = _bf16(jnp.where(causal, att, 0.0))
        st = state_ref[hd]
        oh = _dot(att, v[:, vs]) + _dot(q_dec[:, ks], _bf16(st))
        kv = lax.dot_general(k_end[:, ks], v[:, vs], (((0,), (0,)), ((), ())),
                             preferred_element_type=jnp.float32)
        decay_col = jnp.broadcast_to(block_decay[:, ks], (8, GLA_DK)).T[:, 0:1]
        state_ref[hd] = st * decay_col + kv
        oh = oh * lax.rsqrt(jnp.mean(oh * oh, axis=-1, keepdims=True) + EPS) * gla_norm_g[:, vs]
        o_heads.append(_bf16(oh * _silu(r[:, vs])))
    return jnp.concatenate(o_heads, axis=-1)


def _cast_w_in_blocks(f32_refs, bf16_refs):
    wt_conv_ref, wt_gla_ref = f32_refs[:2]
    o_gla_ref, o_conv_ref = bf16_refs[:2]
    o_conv_ref[...] = _bf16(wt_conv_ref[0].T)
    o_gla_ref[...] = _bf16(wt_gla_ref[...].T)


def _cast_square_rows(f32_refs, bf16_refs):
    for src_ref, dst_ref in zip(f32_refs[2:], bf16_refs[2:]):
        dst_ref[...] = _bf16(src_ref[...])


def _cast_kernel(*refs):
    _cast_w_in_blocks(refs[:N_CAST_IN], refs[N_CAST_IN:])
    _cast_square_rows(refs[:N_CAST_IN], refs[N_CAST_IN:])


def _layer_body(x, weights, state_ref, tail_ref, *, gla_block, chunk, cast_next=None):
    (norm_g_ref, w_gate_up_ref, b_gate_ref, gla_norm_g_ref, conv_w_ref,
     w_gla_ref, w_conv_ref, w_o_gla_ref, w_o_conv_ref, w_out_ref) = weights
    tile = x.shape[0]
    h = x * lax.rsqrt(jnp.mean(x * x, axis=-1, keepdims=True) + EPS) * norm_g_ref[...]
    h = _bf16(h)

    def proj(name):
        if name in _GLA_SECTIONS:
            w_ref, i = w_gla_ref, _GLA_SECTIONS.index(name)
        else:
            w_ref, i = w_conv_ref, _CONV_SECTIONS.index(name)
        return _dot(h, w_ref[:, i * D_MODEL:(i + 1) * D_MODEL])

    r_glr = _dot(h, w_gla_ref[:, 2 * D_MODEL:])
    r = r_glr[:, :D_MODEL]
    glr = r_glr[:, D_MODEL:D_MODEL + GATE_RANK]
    qk = proj("qk")
    z = _dot(_bf16(glr), w_gate_up_ref[...]) + b_gate_ref[...]
    g = _log_sigmoid(z) * (1.0 / GATE_TAU)
    blocks = [slice(s * gla_block, (s + 1) * gla_block) for s in range(tile // gla_block)]
    causal, tril = _gla_masks(gla_block, chunk)
    gla_norm_g = gla_norm_g_ref[...]

    def prep(blk):
        return _gla_prep(qk[blk, :GLA_K], qk[blk, GLA_K:], g[blk, :], tril, chunk)

    operands = prep(blocks[0])
    v = _bf16(proj("v"))
    gated = []
    for s, blk in enumerate(blocks):
        gated.append(_gla_attend(operands, v[blk, :], r[blk, :], causal, gla_norm_g, state_ref))
        if s + 1 < len(blocks):
            operands = prep(blocks[s + 1])
        if s == 0:
            cc = proj("cc")
            ch = proj("ch")
            cb = proj("cb")
    cz = proj("cz")
    if cast_next is not None:
        cast_next()
    y_gla = _dot(jnp.concatenate(gated, axis=0), w_o_gla_ref[...])

    merge_a = _sigmoid(proj("ga"))
    merge_b = _sigmoid(proj("gb"))

    u = cc * ch
    tail = tail_ref[...]
    tok = lax.broadcasted_iota(jnp.int32, (tile, D_MODEL), 0)
    u1 = jnp.where(tok == 0, tail[1:2, :], pltpu.roll(u, 1, 0))
    u2 = jnp.where(tok == 0, tail[0:1, :],
                   jnp.where(tok == 1, tail[1:2, :], pltpu.roll(u, 2, 0)))
    tail_ref[...] = u[tile - 2:tile, :]
    cw = conv_w_ref[...]
    y_c = cw[0:1, :] * u2 + cw[1:2, :] * u1 + cw[2:3, :] * u
    y_c = cb * y_c * _silu(cz)
    y_conv = _dot(_bf16(y_c), w_o_conv_ref[...])

    merged = merge_a * y_gla + merge_b * y_conv
    return x + _dot(_bf16(merged), w_out_ref[...])


N_LAYER_WEIGHTS = 10


def _layer_kernel(*refs, tile, n_tiles, n_steps, gla_block, chunk, apply_final_norm, convert_next):
    x_hbm_ref, meta_ref, final_g_ref = refs[:3]
    weights = refs[3:3 + N_LAYER_WEIGHTS]
    n_in = 3 + N_LAYER_WEIGHTS + (N_CAST_IN if convert_next else 0)
    cast_in = refs[3 + N_LAYER_WEIGHTS:n_in]
    y_ref, meta_out_ref = refs[n_in:n_in + 2]
    cast_out = refs[n_in + 2:-6]
    state_ref, tail_ref, meta_state_ref, meta_tail_ref, x_buf_ref, x_sem_ref = refs[-6:]
    t = pl.program_id(1)
    step = pl.program_id(0) * n_tiles + t

    def x_copy(s):
        slot = lax.rem(s, X_RING_SLOTS)
        rows = pl.ds(pl.multiple_of(lax.rem(s, n_tiles) * tile, tile), tile)
        return pltpu.make_async_copy(x_hbm_ref.at[s // n_tiles, rows, :], x_buf_ref.at[slot],
                                     x_sem_ref.at[slot])

    @pl.when(step == 0)
    def _():
        for s in range(min(X_RING_SLOTS - 1, n_steps)):
            x_copy(s).start()

    @pl.when(step + (X_RING_SLOTS - 1) < n_steps)
    def _():
        x_copy(step + (X_RING_SLOTS - 1)).start()

    @pl.when(step == 0)
    def _():
        state_ref[...] = jnp.zeros(state_ref.shape, jnp.float32)
        tail_ref[...] = jnp.zeros(tail_ref.shape, jnp.float32)
        meta_out_ref[...] = _layer_body(meta_ref[...], weights, state_ref, tail_ref,
                                        gla_block=N_META, chunk=N_META)
        meta_state_ref[...] = state_ref[...]
        meta_tail_ref[...] = tail_ref[...]

    @pl.when(t == 0)
    def _():
        state_ref[...] = meta_state_ref[...]
        tail_ref[...] = meta_tail_ref[...]

    def cast_next():
        _cast_square_rows(cast_in, cast_out)
        _cast_w_in_blocks(cast_in, cast_out)

    x_copy(step).wait()
    y = _layer_body(x_buf_ref[lax.rem(step, X_RING_SLOTS)], weights, state_ref, tail_ref,
                    gla_block=gla_block, chunk=chunk,
                    cast_next=cast_next if convert_next else None)
    if apply_final_norm:
        y = y * lax.rsqrt(jnp.mean(y * y, axis=-1, keepdims=True) + EPS) * final_g_ref[...]
    y_ref[...] = y


def _resident(shape):
    return pl.BlockSpec(shape, lambda b, t: (0,) * len(shape), pipeline_mode=pl.Buffered(1))


def _layer_slice(layer, shape):
    return pl.BlockSpec((None,) + shape, lambda b, t: (layer,) + (0,) * len(shape),
                        pipeline_mode=pl.Buffered(1))


N_CAST_IN = 5
_CAST_COLS = (N_GLA_COLS + LANES, N_CONV_COLS, D_MODEL, D_MODEL, D_MODEL)
F32_ROWS = 8
_CAST_BLOCKS_INLINE = (256, LANES)
_CAST_BLOCKS_ALONE = (1024, 640, 256)


def _cast_steps(conv_cols, gla_cols):
    assert N_CONV_COLS % conv_cols == 0 and (N_GLA_COLS + LANES) % gla_cols == 0
    return max(N_CONV_COLS // conv_cols, (N_GLA_COLS + LANES) // gla_cols)


def _cast_specs(layer, step_of, conv_cols, gla_cols, square_rows, square_block):
    assert square_rows % BF16_ROWS == 0 and (N_GLA_COLS + GATE_RANK) % F32_ROWS == 0
    n_conv = N_CONV_COLS // conv_cols
    n_gla = (N_GLA_COLS + LANES) // gla_cols

    def conv_block(*g):
        return jnp.minimum(step_of(*g), n_conv - 1)

    def gla_block(*g):
        return jnp.minimum(step_of(*g), n_gla - 1)

    def conv_window(*g):
        row = N_GLA_COLS + GATE_RANK + conv_cols * conv_block(*g)
        return (layer, pl.multiple_of(row, F32_ROWS), 0)

    in_specs = [
        pl.BlockSpec((pl.Element(1), pl.Element(conv_cols), pl.Element(D_MODEL)), conv_window),
        pl.BlockSpec((None, gla_cols, D_MODEL), lambda *g: (layer, gla_block(*g), 0)),
    ] + [pl.BlockSpec((None, square_rows, D_MODEL), lambda *g: (layer, square_block(*g), 0))] * 3
    out_specs = [
        pl.BlockSpec((D_MODEL, gla_cols), lambda *g: (0, gla_block(*g))),
        pl.BlockSpec((D_MODEL, conv_cols), lambda *g: (0, conv_block(*g))),
    ] + [pl.BlockSpec((square_rows, D_MODEL), lambda *g: (square_block(*g), 0))] * 3
    out_shape = [jax.ShapeDtypeStruct((D_MODEL, cols), jnp.bfloat16) for cols in _CAST_COLS]
    return in_specs, out_specs, out_shape


def _cast_layer(f32_mats, layer):
    conv_cols, gla_cols, square_rows = _CAST_BLOCKS_ALONE
    n_square = D_MODEL // square_rows
    in_specs, out_specs, out_shape = _cast_specs(
        layer, lambda i: i, conv_cols, gla_cols, square_rows,
        lambda i: jnp.minimum(i, n_square - 1))
    return pl.pallas_call(
        _cast_kernel,
        grid=(max(_cast_steps(conv_cols, gla_cols), n_square),),
        in_specs=in_specs,
        out_specs=out_specs,
        out_shape=out_shape,
        compiler_params=pltpu.CompilerParams(dimension_semantics=("arbitrary",),
                                             vmem_limit_bytes=VMEM_LIMIT_BYTES),
        name="cast_weights",
    )(*f32_mats)


def _layer_call(x, meta_h, small, big, final_g, layer, next_f32, *, tile, chunk,
                apply_final_norm):
    bsz, seq, _ = x.shape
    n_tiles = seq // tile
    gla_block = min(tile, GLA_BLOCK)
    assert seq % tile == 0 and tile % gla_block == 0 and gla_block % chunk == 0
    convert_next = next_f32 is not None
    kern = functools.partial(_layer_kernel, tile=tile, n_tiles=n_tiles, n_steps=bsz * n_tiles,
                             gla_block=gla_block, chunk=chunk,
                             apply_final_norm=apply_final_norm, convert_next=convert_next)
    tok_spec = pl.BlockSpec((None, tile, D_MODEL), lambda b, t: (b, t, 0))
    state_shape = (GLA_HEADS, GLA_DK, GLA_DV)
    tail_shape = (CONV_K - 1, D_MODEL)
    in_specs = [
        pl.BlockSpec(memory_space=pl.ANY),
        _resident((N_META, D_MODEL)),
        _resident((1, D_MODEL)),
        _layer_slice(layer, (1, D_MODEL)),
        _layer_slice(layer, (GATE_RANK, GLA_K)),
        _layer_slice(layer, (1, GLA_K)),
        _layer_slice(layer, (1, GLA_V)),
        _layer_slice(layer, (CONV_K, D_MODEL)),
        _resident((D_MODEL, N_GLA_COLS + LANES)),
        _resident((D_MODEL, N_CONV_COLS)),
        _resident((GLA_V, D_MODEL)),
        _resident((D_MODEL, D_MODEL)),
        _resident((D_MODEL, D_MODEL)),
    ]
    out_specs = [tok_spec, pl.BlockSpec((N_META, D_MODEL), lambda b, t: (0, 0))]
    out_shape = [jax.ShapeDtypeStruct(x.shape, jnp.float32),
                 jax.ShapeDtypeStruct((N_META, D_MODEL), jnp.float32)]
    args = [x, meta_h, final_g, *small, *big]
    if convert_next:
        steps = bsz * n_tiles
        assert D_MODEL % steps == 0
        assert steps >= _cast_steps(*_CAST_BLOCKS_INLINE)

        def step_of(b, t):
            return b * n_tiles + t

        cast = _cast_specs(layer + 1, step_of, *_CAST_BLOCKS_INLINE, D_MODEL // steps, step_of)
        in_specs.extend(cast[0])
        out_specs.extend(cast[1])
        out_shape.extend(cast[2])
        args.extend(next_f32)
    return pl.pallas_call(
        kern,
        grid=(bsz, n_tiles),
        in_specs=in_specs,
        out_specs=out_specs,
        out_shape=out_shape,
        scratch_shapes=[
            pltpu.VMEM(state_shape, jnp.float32),
            pltpu.VMEM(tail_shape, jnp.float32),
            pltpu.VMEM(state_shape, jnp.float32),
            pltpu.VMEM(tail_shape, jnp.float32),
            pltpu.VMEM((X_RING_SLOTS, tile, D_MODEL), jnp.float32),
            pltpu.SemaphoreType.DMA((X_RING_SLOTS,)),
        ],
        compiler_params=pltpu.CompilerParams(
            dimension_semantics=("arbitrary", "arbitrary"),
            vmem_limit_bytes=VMEM_LIMIT_BYTES),
        name="hybrid_layer",
    )(*args)


def kernel(x, meta, norm_g, w_in, w_gate_up, b_gate, gla_norm_g, w_o_gla, conv_w, w_o_conv, w_out,
           final_norm_g):
    depth = w_in.shape[0]
    small = (norm_g.reshape(depth, 1, D_MODEL), _bf16(w_gate_up), b_gate.reshape(depth, 1, GLA_K),
             gla_norm_g.reshape(depth, 1, GLA_V), conv_w)
    w_in_t = jnp.swapaxes(w_in, 1, 2)
    next_f32 = (w_in_t, w_in_t, w_o_gla, w_o_conv, w_out)
    big = _cast_layer(next_f32, 0)
    final_g = final_norm_g.reshape(1, D_MODEL)
    meta_h = meta.astype(x.dtype)
    h = x
    for l in range(depth):
        last = l == depth - 1
        h, meta_h, *big = _layer_call(h, meta_h, small, big, final_g, l, None if last else next_f32,
                                      tile=TOKEN_TILE, chunk=CHUNK, apply_final_norm=last)
    return h
```

```python
import functools

import jax
import jax.numpy as jnp
from jax import lax
from jax.experimental import pallas as pl
from jax.experimental.pallas import tpu as pltpu

D_MODEL = 1024
N_META = 16
GLA_HEADS = 4
GLA_K = D_MODEL // 2
GLA_V = D_MODEL
GLA_DK = GLA_K // GLA_HEADS
GLA_DV = GLA_V // GLA_HEADS
GATE_RANK = 16
GATE_TAU = 16.0
CHUNK = 64
CONV_K = 3
EPS = 1e-6
LANES = 128
BF16_ROWS = 16

_GLA_SECTIONS = ("qk", "v", "r")
_CONV_SECTIONS = ("ch", "cb", "cc", "cz", "ga", "gb")
N_GLA_COLS = len(_GLA_SECTIONS) * D_MODEL
N_CONV_COLS = len(_CONV_SECTIONS) * D_MODEL

VMEM_LIMIT_BYTES = 60 * 1024 * 1024
TOKEN_TILE = 512
GLA_BLOCK = 256
CONV_HALF = 512


def _sigmoid(z):
    return 0.5 * jnp.tanh(0.5 * z) + 0.5


def _silu(z):
    return z * _sigmoid(z)


def _log_sigmoid(z):
    return jnp.minimum(z, 0.0) - jnp.log(1.0 + jnp.exp(-jnp.abs(z)))


def _bf16(a):
    return a.astype(jnp.bfloat16)


def _dot(a, b):
    return jnp.dot(a, b, preferred_element_type=jnp.float32)


def _split2_bf16(a):
    hi = _bf16(a)
    lo = _bf16(a - hi.astype(jnp.float32))
    return hi, lo


def _gla_masks(block, chunk):
    ti = lax.broadcasted_iota(jnp.int32, (block, block), 0)
    si = lax.broadcasted_iota(jnp.int32, (block, block), 1)
    causal = ti >= si
    shift = chunk.bit_length() - 1
    same_chunk = lax.shift_right_logical(ti, shift) == lax.shift_right_logical(si, shift)
    return causal, _bf16((causal & same_chunk).astype(jnp.float32))


def _gla_prep(q, k, g, tril, chunk):
    block = q.shape[0]
    n_chunks = block // chunk
    g_hi, g_lo = _split2_bf16(g)
    b = _dot(tril, g_hi) + _dot(tril, g_lo)

    def rows(c):
        return slice(c * chunk, (c + 1) * chunk)

    b_last = [b[(c + 1) * chunk - 1:(c + 1) * chunk, :] for c in range(n_chunks)]
    base = [jnp.zeros((1, GLA_K), jnp.float32)]
    for c in range(n_chunks):
        base.append(base[c] + b_last[c])

    q_in = q * (jnp.exp(b) * (GLA_DK ** -0.5))
    k_in = k * jnp.exp(-b)
    q_in_c, q_dec_c, k_in_c, k_st_c, k_end_c = [], [], [], [], []
    for c in range(n_chunks):
        q_c = q_in[rows(c), :]
        q_in_c.append(_bf16(q_c))
        q_dec_c.append(_bf16(q_c * jnp.exp(base[c])))
        k_in_c.append(_bf16(k_in[rows(c), :]))
        k_st = k[rows(c), :] * jnp.exp(b_last[c] - b[rows(c), :])
        k_st_c.append(k_st)
        k_end_c.append(_bf16(k_st * jnp.exp(base[n_chunks] - base[c + 1])))
    q_dec = jnp.concatenate(q_dec_c, axis=0)
    k_end = jnp.concatenate(k_end_c, axis=0)
    block_decay = jnp.exp(base[n_chunks])

    zero_rows = jnp.zeros((chunk, GLA_K), jnp.bfloat16)
    k_seen = []
    for c in range(n_chunks):
        blocks = [_bf16(k_st_c[j] * jnp.exp(base[c] - base[j + 1])) for j in range(c)]
        blocks.append(k_in_c[c])
        blocks.extend([zero_rows] * (n_chunks - 1 - c))
        k_seen.append(jnp.concatenate(blocks, axis=0))
    return q_in_c, k_seen, q_dec, k_end, block_decay


def _gla_attend(operands, v, r, causal, gla_norm_g, state_ref):
    q_in_c, k_seen, q_dec, k_end, block_decay = operands
    n_chunks = len(q_in_c)
    nt_dims = (((1,), (1,)), ((), ()))
    o_heads = []
    for hd in range(GLA_HEADS):
        ks = slice(hd * GLA_DK, (hd + 1) * GLA_DK)
        vs = slice(hd * GLA_DV, (hd + 1) * GLA_DV)
        att = jnp.concatenate(
            [lax.dot_general(q_in_c[c][:, ks], k_seen[c][:, ks], nt_dims,
                             preferred_element_type=jnp.float32) for c in range(n_chunks)],
            axis=0)
        att = _bf16(jnp.where(causal, att, 0.0))
        st = state_ref[hd]
        oh = _dot(att, v[:, vs]) + _dot(q_dec[:, ks], _bf16(st))
        kv = lax.dot_general(k_end[:, ks], v[:, vs], (((0,), (0,)), ((), ())),
                             preferred_element_type=jnp.float32)
        decay_col = jnp.broadcast_to(block_decay[:, ks], (8, GLA_DK)).T[:, 0:1]
        state_ref[hd] = st * decay_col + kv
        oh = oh * lax.rsqrt(jnp.mean(oh * oh, axis=-1, keepdims=True) + EPS) * gla_norm_g[:, vs]
        o_heads.append(_bf16(oh * _silu(r[:, vs])))
    return jnp.concatenate(o_heads, axis=-1)


def _cast_w_in_blocks(f32_refs, bf16_refs):
    wt_conv_ref, wt_gla_ref = f32_refs[:2]
    o_gla_ref, o_conv_ref = bf16_refs[:2]
    o_conv_ref[...] = _bf16(wt_conv_ref[0].T)
    o_gla_ref[...] = _bf16(wt_gla_ref[...].T)


def _cast_square_rows(f32_refs, bf16_refs):
    for src_ref, dst_ref in zip(f32_refs[2:], bf16_refs[2:]):
        dst_ref[...] = _bf16(src_ref[...])


def _cast_kernel(*refs):
    _cast_w_in_blocks(refs[:N_CAST_IN], refs[N_CAST_IN:])
    _cast_square_rows(refs[:N_CAST_IN], refs[N_CAST_IN:])


def _layer_body(x, weights, state_ref, tail_ref, *, gla_block, chunk, cast_next=None):
    (norm_g_ref, w_gate_up_ref, b_gate_ref, gla_norm_g_ref, conv_w_ref,
     w_gla_ref, w_conv_ref, w_o_gla_ref, w_o_conv_ref, w_out_ref) = weights
    tile = x.shape[0]
    h = x * lax.rsqrt(jnp.mean(x * x, axis=-1, keepdims=True) + EPS) * norm_g_ref[...]
    h = _bf16(h)

    def proj(name):
        if name in _GLA_SECTIONS:
            w_ref, i = w_gla_ref, _GLA_SECTIONS.index(name)
        else:
            w_ref, i = w_conv_ref, _CONV_SECTIONS.index(name)
        return _dot(h, w_ref[:, i * D_MODEL:(i + 1) * D_MODEL])

    r_glr = _dot(h, w_gla_ref[:, 2 * D_MODEL:])
    r = r_glr[:, :D_MODEL]
    glr = r_glr[:, D_MODEL:D_MODEL + GATE_RANK]
    qk = proj("qk")
    z = _dot(_bf16(glr), w_gate_up_ref[...]) + b_gate_ref[...]
    g = _log_sigmoid(z) * (1.0 / GATE_TAU)
    blocks = [slice(s * gla_block, (s + 1) * gla_block) for s in range(tile // gla_block)]
    causal, tril = _gla_masks(gla_block, chunk)
    gla_norm_g = gla_norm_g_ref[...]

    def prep(blk):
        return _gla_prep(qk[blk, :GLA_K], qk[blk, GLA_K:], g[blk, :], tril, chunk)

    tail = tail_ref[...]
    tok = lax.broadcasted_iota(jnp.int32, (tile, CONV_HALF), 0)
    cw = conv_w_ref[...]

    def conv_half(half):
        cols = slice(half * CONV_HALF, (half + 1) * CONV_HALF)

        def proj_half(name):
            i = _CONV_SECTIONS.index(name)
            return _dot(h, w_conv_ref[:, i * D_MODEL + half * CONV_HALF:
                                      i * D_MODEL + (half + 1) * CONV_HALF])

        u = proj_half("cc") * proj_half("ch")
        u1 = jnp.where(tok == 0, tail[1:2, cols], pltpu.roll(u, 1, 0))
        u2 = jnp.where(tok == 0, tail[0:1, cols],
                       jnp.where(tok == 1, tail[1:2, cols], pltpu.roll(u, 2, 0)))
        tail_ref[:, cols] = u[tile - 2:tile, :]
        y_c = cw[0:1, cols] * u2 + cw[1:2, cols] * u1 + cw[2:3, cols] * u
        y_c = proj_half("cb") * y_c * _silu(proj_half("cz"))
        return _dot(_bf16(y_c), w_o_conv_ref[cols, :])

    operands = prep(blocks[0])
    v = _bf16(proj("v"))
    gated = []
    for s, blk in enumerate(blocks):
        gated.append(_gla_attend(operands, v[blk, :], r[blk, :], causal, gla_norm_g, state_ref))
        if s + 1 < len(blocks):
            operands = prep(blocks[s + 1])
        if s == 0:
            y_conv = conv_half(0)
    y_conv = y_conv + conv_half(1)
    if cast_next is not None:
        cast_next()
    y_gla = _dot(jnp.concatenate(gated, axis=0), w_o_gla_ref[...])

    merge_a = _sigmoid(proj("ga"))
    merge_b = _sigmoid(proj("gb"))

    merged = merge_a * y_gla + merge_b * y_conv
    return x + _dot(_bf16(merged), w_out_ref[...])


N_LAYER_WEIGHTS = 10


def _layer_kernel(*refs, gla_block, chunk, apply_final_norm, convert_next):
    x_ref, meta_ref, final_g_ref = refs[:3]
    weights = refs[3:3 + N_LAYER_WEIGHTS]
    n_in = 3 + N_LAYER_WEIGHTS + (N_CAST_IN if convert_next else 0)
    cast_in = refs[3 + N_LAYER_WEIGHTS:n_in]
    y_ref, meta_out_ref = refs[n_in:n_in + 2]
    cast_out = refs[n_in + 2:-4]
    state_ref, tail_ref, meta_state_ref, meta_tail_ref = refs[-4:]
    t = pl.program_id(1)

    @pl.when(pl.program_id(0) * pl.num_programs(1) + t == 0)
    def _():
        state_ref[...] = jnp.zeros(state_ref.shape, jnp.float32)
        tail_ref[...] = jnp.zeros(tail_ref.shape, jnp.float32)
        meta_out_ref[...] = _layer_body(meta_ref[...], weights, state_ref, tail_ref,
                                        gla_block=N_META, chunk=N_META)
        meta_state_ref[...] = state_ref[...]
        meta_tail_ref[...] = tail_ref[...]

    @pl.when(t == 0)
    def _():
        state_ref[...] = meta_state_ref[...]
        tail_ref[...] = meta_tail_ref[...]

    def cast_next():
        _cast_square_rows(cast_in, cast_out)
        _cast_w_in_blocks(cast_in, cast_out)

    y = _layer_body(x_ref[...], weights, state_ref, tail_ref, gla_block=gla_block, chunk=chunk,
                    cast_next=cast_next if convert_next else None)
    if apply_final_norm:
        y = y * lax.rsqrt(jnp.mean(y * y, axis=-1, keepdims=True) + EPS) * final_g_ref[...]
    y_ref[...] = y


def _resident(shape):
    return pl.BlockSpec(shape, lambda b, t: (0,) * len(shape), pipeline_mode=pl.Buffered(1))


def _layer_slice(layer, shape):
    return pl.BlockSpec((None,) + shape, lambda b, t: (layer,) + (0,) * len(shape),
                        pipeline_mode=pl.Buffered(1))


N_CAST_IN = 5
_CAST_COLS = (N_GLA_COLS + LANES, N_CONV_COLS, D_MODEL, D_MODEL, D_MODEL)
F32_ROWS = 8
_CAST_BLOCKS_INLINE = (256, LANES)
_CAST_BLOCKS_ALONE = (1024, 640, 256)


def _cast_steps(conv_cols, gla_cols):
    assert N_CONV_COLS % conv_cols == 0 and (N_GLA_COLS + LANES) % gla_cols == 0
    return max(N_CONV_COLS // conv_cols, (N_GLA_COLS + LANES) // gla_cols)


def _cast_specs(layer, step_of, conv_cols, gla_cols, square_rows, square_block):
    assert square_rows % BF16_ROWS == 0 and (N_GLA_COLS + GATE_RANK) % F32_ROWS == 0
    n_conv = N_CONV_COLS // conv_cols
    n_gla = (N_GLA_COLS + LANES) // gla_cols

    def conv_block(*g):
        return jnp.minimum(step_of(*g), n_conv - 1)

    def gla_block(*g):
        return jnp.minimum(step_of(*g), n_gla - 1)

    def conv_window(*g):
        row = N_GLA_COLS + GATE_RANK + conv_cols * conv_block(*g)
        return (layer, pl.multiple_of(row, F32_ROWS), 0)

    in_specs = [
        pl.BlockSpec((pl.Element(1), pl.Element(conv_cols), pl.Element(D_MODEL)), conv_window),
        pl.BlockSpec((None, gla_cols, D_MODEL), lambda *g: (layer, gla_block(*g), 0)),
    ] + [pl.BlockSpec((None, square_rows, D_MODEL), lambda *g: (layer, square_block(*g), 0))] * 3
    out_specs = [
        pl.BlockSpec((D_MODEL, gla_cols), lambda *g: (0, gla_block(*g))),
        pl.BlockSpec((D_MODEL, conv_cols), lambda *g: (0, conv_block(*g))),
    ] + [pl.BlockSpec((square_rows, D_MODEL), lambda *g: (square_block(*g), 0))] * 3
    out_shape = [jax.ShapeDtypeStruct((D_MODEL, cols), jnp.bfloat16) for cols in _CAST_COLS]
    return in_specs, out_specs, out_shape


def _cast_layer(f32_mats, layer):
    conv_cols, gla_cols, square_rows = _CAST_BLOCKS_ALONE
    n_square = D_MODEL // square_rows
    in_specs, out_specs, out_shape = _cast_specs(
        layer, lambda i: i, conv_cols, gla_cols, square_rows,
        lambda i: jnp.minimum(i, n_square - 1))
    return pl.pallas_call(
        _cast_kernel,
        grid=(max(_cast_steps(conv_cols, gla_cols), n_square),),
        in_specs=in_specs,
        out_specs=out_specs,
        out_shape=out_shape,
        compiler_params=pltpu.CompilerParams(dimension_semantics=("arbitrary",),
                                             vmem_limit_bytes=VMEM_LIMIT_BYTES),
        name="cast_weights",
    )(*f32_mats)


def _layer_call(x, meta_h, small, big, final_g, layer, next_f32, *, tile, chunk,
                apply_final_norm):
    bsz, seq, _ = x.shape
    n_tiles = seq // tile
    gla_block = min(tile, GLA_BLOCK)
    assert seq % tile == 0 and tile % gla_block == 0 and gla_block % chunk == 0
    convert_next = next_f32 is not None
    kern = functools.partial(_layer_kernel, gla_block=gla_block, chunk=chunk,
                             apply_final_norm=apply_final_norm, convert_next=convert_next)
    tok_spec = pl.BlockSpec((None, tile, D_MODEL), lambda b, t: (b, t, 0))
    state_shape = (GLA_HEADS, GLA_DK, GLA_DV)
    tail_shape = (CONV_K - 1, D_MODEL)
    in_specs = [
        tok_spec,
        _resident((N_META, D_MODEL)),
        _resident((1, D_MODEL)),
        _layer_slice(layer, (1, D_MODEL)),
        _layer_slice(layer, (GATE_RANK, GLA_K)),
        _layer_slice(layer, (1, GLA_K)),
        _layer_slice(layer, (1, GLA_V)),
        _layer_slice(layer, (CONV_K, D_MODEL)),
        _resident((D_MODEL, N_GLA_COLS + LANES)),
        _resident((D_MODEL, N_CONV_COLS)),
        _resident((GLA_V, D_MODEL)),
        _resident((D_MODEL, D_MODEL)),
        _resident((D_MODEL, D_MODEL)),
    ]
    out_specs = [tok_spec, pl.BlockSpec((N_META, D_MODEL), lambda b, t: (0, 0))]
    out_shape = [jax.ShapeDtypeStruct(x.shape, jnp.float32),
                 jax.ShapeDtypeStruct((N_META, D_MODEL), jnp.float32)]
    args = [x, meta_h, final_g, *small, *big]
    if convert_next:
        steps = bsz * n_tiles
        assert D_MODEL % steps == 0
        assert steps >= _cast_steps(*_CAST_BLOCKS_INLINE)

        def step_of(b, t):
            return b * n_tiles + t

        cast = _cast_specs(layer + 1, step_of, *_CAST_BLOCKS_INLINE, D_MODEL // steps, step_of)
        in_specs.extend(cast[0])
        out_specs.extend(cast[1])
        out_shape.extend(cast[2])
        args.extend(next_f32)
    return pl.pallas_call(
        kern,
        grid=(bsz, n_tiles),
        in_specs=in_specs,
        out_specs=out_specs,
        out_shape=out_shape,
        scratch_shapes=[
            pltpu.VMEM(state_shape, jnp.float32),
            pltpu.VMEM(tail_shape, jnp.float32),
            pltpu.VMEM(state_shape, jnp.float32),
            pltpu.VMEM(tail_shape, jnp.float32),
        ],
        compiler_params=pltpu.CompilerParams(
            dimension_semantics=("arbitrary", "arbitrary"),
            vmem_limit_bytes=VMEM_LIMIT_BYTES),
        name="hybrid_layer",
    )(*args)


def kernel(x, meta, norm_g, w_in, w_gate_up, b_gate, gla_norm_g, w_o_gla, conv_w, w_o_conv, w_out,
           final_norm_g):
    depth = w_in.shape[0]
    small = (norm_g.reshape(depth, 1, D_MODEL), _bf16(w_gate_up), b_gate.reshape(depth, 1, GLA_K),
             gla_norm_g.reshape(depth, 1, GLA_V), conv_w)
    w_in_t = jnp.swapaxes(w_in, 1, 2)
    next_f32 = (w_in_t, w_in_t, w_o_gla, w_o_conv, w_out)
    big = _cast_layer(next_f32, 0)
    final_g = final_norm_g.reshape(1, D_MODEL)
    meta_h = meta.astype(x.dtype)
    h = x
    for l in range(depth):
        last = l == depth - 1
        h, meta_h, *big = _layer_call(h, meta_h, small, big, final_g, l, None if last else next_f32,
                                      tile=TOKEN_TILE, chunk=CHUNK, apply_final_norm=last)
    return h
```
